```python
import math
import jax, jax.numpy as jnp
from jax import lax
import numpy as np

D_MODEL = 2048
BATCH = 4
SEQ = 4096
DEPTH = 1

MIX_WIDTH = D_MODEL
SSM_WIDTH = MIX_WIDTH // 2
POOL_WIDTH = MIX_WIDTH - SSM_WIDTH
SSM_GROUP = 16
SSM_GROUPS = SSM_WIDTH // SSM_GROUP
SSM_STATE = 64
POOL_WINDOWS = (2, 4, 8, 16)
POOL_GROUPS = len(POOL_WINDOWS)
POOL_GROUP_WIDTH = POOL_WIDTH // POOL_GROUPS
N_MEM = 256
MEM_HEADS = 4
MEM_HEAD_DIM = D_MODEL // MEM_HEADS
D_FF = ((8 * D_MODEL // 3 + 255) // 256) * 256
DT_MIN = 1e-3
DT_MAX = 1e-1
EPS = 1e-6

kernel_name = 'hymba_s5_pool_macaron_block'

F32 = jnp.float32


def rmsnorm(x, g):
    xf = x.astype(F32)
    y = xf * lax.rsqrt(jnp.mean(xf * xf, axis=-1, keepdims=True) + EPS) * g.astype(F32)
    return y.astype(x.dtype)


def swiglu(x, w_gate, w_up, w_down):
    return (jax.nn.silu(x @ w_gate) * (x @ w_up)) @ w_down


def _complex_scan_combine(e1, e2):
    a1r, a1i, b1r, b1i = e1
    a2r, a2i, b2r, b2i = e2
    ar = a2r * a1r - a2i * a1i
    ai = a2r * a1i + a2i * a1r
    br = a2r * b1r - a2i * b1i + b2r
    bi = a2r * b1i + a2i * b1r + b2i
    return (ar, ai, br, bi)


def s5_mixer(u, a_re, a_im, log_dt, b_re, b_im, c_re, c_im, d_skip, w_glu, b_glu):
    bsz, L, _ = u.shape
    uf = u.astype(F32)
    ug = uf.reshape(bsz, L, SSM_GROUPS, SSM_GROUP)
    dt = jnp.exp(log_dt.astype(F32))[:, None]
    lr, li = a_re.astype(F32), a_im.astype(F32)
    mag = jnp.exp(lr * dt)
    abar_re = mag * jnp.cos(li * dt)
    abar_im = mag * jnp.sin(li * dt)
    nr, ni = abar_re - 1.0, abar_im
    den = lr * lr + li * li
    fr = (nr * lr + ni * li) / den
    fi = (ni * lr - nr * li) / den
    br, bi = b_re.astype(F32), b_im.astype(F32)
    bbar_re = fr[..., None] * br - fi[..., None] * bi
    bbar_im = fr[..., None] * bi + fi[..., None] * br
    bu_re = jnp.einsum('blgh,gph->blgp', ug, bbar_re)
    bu_im = jnp.einsum('blgh,gph->blgp', ug, bbar_im)
    a_full_re = jnp.broadcast_to(abar_re, bu_re.shape)
    a_full_im = jnp.broadcast_to(abar_im, bu_im.shape)
    _, _, s_re, s_im = lax.associative_scan(
        _complex_scan_combine, (a_full_re, a_full_im, bu_re, bu_im), axis=1)
    y = (jnp.einsum('blgp,ghp->blgh', s_re, c_re.astype(F32))
         - jnp.einsum('blgp,ghp->blgh', s_im, c_im.astype(F32)))
    y = y.reshape(bsz, L, SSM_WIDTH) + d_skip.astype(F32) * uf
    y = jax.nn.gelu(y)
    y = y * jax.nn.sigmoid(y @ w_glu.astype(F32) + b_glu.astype(F32))
    return y.astype(u.dtype)


def pool_mixer(v, w_pool, pool_scale):
    bsz, L, _ = v.shape
    vf = v.astype(F32).reshape(bsz, L, POOL_GROUPS, POOL_GROUP_WIDTH)
    csum = jnp.cumsum(vf, axis=1)
    t = jnp.arange(L)
    pooled = []
    for gi, w in enumerate(POOL_WINDOWS):
        cg = csum[:, :, gi]
        shifted = jnp.pad(cg, ((0, 0), (w, 0), (0, 0)))[:, :L]
        cnt = jnp.minimum(t + 1, w).astype(F32)[None, :, None]
        pooled.append((cg - shifted) / cnt)
    pooled = jnp.stack(pooled, axis=2) - vf
    z = jnp.einsum('blgc,gcd->blgd', pooled, w_pool.astype(F32))
    z = z.reshape(bsz, L, POOL_WIDTH) * pool_scale.astype(F32)
    return z.astype(v.dtype)


def memory_cross_attention(h, memn, w_q, w_k, w_v, w_o):
    bsz, L, _ = h.shape
    q = (h @ w_q).reshape(bsz, L, MEM_HEADS, MEM_HEAD_DIM)
    k = (memn @ w_k).reshape(bsz, N_MEM, MEM_HEADS, MEM_HEAD_DIM)
    v = (memn @ w_v).reshape(bsz, N_MEM, MEM_HEADS, MEM_HEAD_DIM)
    s = jnp.einsum('blhd,bmhd->bhlm', q.astype(F32), k.astype(F32)) * (MEM_HEAD_DIM ** -0.5)
    p = jax.nn.softmax(s, axis=-1).astype(h.dtype)
    o = jnp.einsum('bhlm,bmhd->blhd', p, v).reshape(bsz, L, D_MODEL)
    return o @ w_o


def setup_inputs(seed: int = 0) -> dict:
    key = jax.random.key(seed)
    ks = iter(jax.random.split(key, 40))
    nrm = lambda shape, scale: jax.random.normal(next(ks), shape, F32) * scale
    gain = lambda shape: 1.0 + 0.02 * jax.random.normal(next(ks), shape, F32)
    Ly = DEPTH
    G, P, H = SSM_GROUPS, SSM_STATE, SSM_GROUP
    inp = {}
    inp['x'] = nrm((BATCH, SEQ, D_MODEL), 1.0)
    inp['mem'] = nrm((BATCH, N_MEM, D_MODEL), 1.0)
    inp['g_ffn1'] = gain((Ly, D_MODEL))
    inp['w1_gate'] = nrm((Ly, D_MODEL, D_FF), D_MODEL ** -0.5)
    inp['w1_up'] = nrm((Ly, D_MODEL, D_FF), D_MODEL ** -0.5)
    inp['w1_down'] = nrm((Ly, D_FF, D_MODEL), D_FF ** -0.5)
    inp['g_mix'] = gain((Ly, D_MODEL))
    inp['w_in'] = nrm((Ly, D_MODEL, MIX_WIDTH), D_MODEL ** -0.5)
    inp['ssm_a_re'] = -0.5 + nrm((Ly, G, P), 0.01)
    inp['ssm_a_im'] = math.pi * jnp.arange(P, dtype=F32)[None, None, :] + nrm((Ly, G, P), 0.01)
    inp['ssm_log_dt'] = jax.random.uniform(next(ks), (Ly, G), F32, math.log(DT_MIN), math.log(DT_MAX))
    inp['ssm_b_re'] = nrm((Ly, G, P, H), (2 * H) ** -0.5)
    inp['ssm_b_im'] = nrm((Ly, G, P, H), (2 * H) ** -0.5)
    inp['ssm_c_re'] = nrm((Ly, G, H, P), (2 * P) ** -0.5)
    inp['ssm_c_im'] = nrm((Ly, G, H, P), (2 * P) ** -0.5)
    inp['ssm_d'] = nrm((Ly, SSM_WIDTH), 1.0)
    inp['w_glu'] = nrm((Ly, SSM_WIDTH, SSM_WIDTH), SSM_WIDTH ** -0.5)
    inp['b_glu'] = nrm((Ly, SSM_WIDTH), 0.02)
    inp['w_pool'] = nrm((Ly, POOL_GROUPS, POOL_GROUP_WIDTH, POOL_GROUP_WIDTH), POOL_GROUP_WIDTH ** -0.5)
    inp['pool_scale'] = 1.0 + nrm((Ly, POOL_WIDTH), 0.1)
    inp['g_out_ssm'] = gain((Ly, SSM_WIDTH))
    inp['g_out_pool'] = gain((Ly, POOL_WIDTH))
    inp['w_out'] = nrm((Ly, MIX_WIDTH, D_MODEL), MIX_WIDTH ** -0.5)
    inp['g_xattn'] = gain((Ly, D_MODEL))
    inp['g_mem'] = gain((Ly, D_MODEL))
    inp['w_q'] = nrm((Ly, D_MODEL, D_MODEL), D_MODEL ** -0.5)
    inp['w_k'] = nrm((Ly, D_MODEL, D_MODEL), D_MODEL ** -0.5)
    inp['w_v'] = nrm((Ly, D_MODEL, D_MODEL), D_MODEL ** -0.5)
    inp['w_o'] = nrm((Ly, D_MODEL, D_MODEL), D_MODEL ** -0.5)
    inp['g_ffn2'] = gain((Ly, D_MODEL))
    inp['w2_gate'] = nrm((Ly, D_MODEL, D_FF), D_MODEL ** -0.5)
    inp['w2_up'] = nrm((Ly, D_MODEL, D_FF), D_MODEL ** -0.5)
    inp['w2_down'] = nrm((Ly, D_FF, D_MODEL), D_FF ** -0.5)
    inp['g_final'] = gain((D_MODEL,))
    return inp


def reference(x, mem, g_ffn1, w1_gate, w1_up, w1_down, g_mix, w_in,
              ssm_a_re, ssm_a_im, ssm_log_dt, ssm_b_re, ssm_b_im, ssm_c_re, ssm_c_im,
              ssm_d, w_glu, b_glu, w_pool, pool_scale, g_out_ssm, g_out_pool, w_out,
              g_xattn, g_mem, w_q, w_k, w_v, w_o,
              g_ffn2, w2_gate, w2_up, w2_down, g_final):
    h = x
    for l in range(DEPTH):
        h = h + 0.5 * swiglu(rmsnorm(h, g_ffn1[l]), w1_gate[l], w1_up[l], w1_down[l])
        u = rmsnorm(h, g_mix[l]) @ w_in[l]
        u_ssm, u_pool = u[..., :SSM_WIDTH], u[..., SSM_WIDTH:]
        y_ssm = s5_mixer(u_ssm, ssm_a_re[l], ssm_a_im[l], ssm_log_dt[l], ssm_b_re[l], ssm_b_im[l],
                         ssm_c_re[l], ssm_c_im[l], ssm_d[l], w_glu[l], b_glu[l])
        y_pool = pool_mixer(u_pool, w_pool[l], pool_scale[l])
        merged = jnp.concatenate([rmsnorm(y_ssm, g_out_ssm[l]), rmsnorm(y_pool, g_out_pool[l])], axis=-1)
        h = h + merged @ w_out[l]
        memn = rmsnorm(mem, g_mem[l])
        h = h + memory_cross_attention(rmsnorm(h, g_xattn[l]), memn, w_q[l], w_k[l], w_v[l], w_o[l])
        h = h + 0.5 * swiglu(rmsnorm(h, g_ffn2[l]), w2_gate[l], w2_up[l], w2_down[l])
    return rmsnorm(h, g_final)
```

```python
import functools
import math

import jax
import jax.numpy as jnp
from jax import lax
from jax.experimental import pallas as pl
from jax.experimental.pallas import tpu as pltpu

F32 = jnp.float32
BF16 = jnp.bfloat16

EPS = 1e-6
SSM_GROUP = 16
SSM_STATE = 64
POOL_WINDOWS = (2, 4, 8, 16)
MEM_HEADS = 4

LANES = 128
SUBLANES = 8
VMEM_BYTES_V7X = 64 * 1024 * 1024
POOL_HALO = 16


def _vmem_limit(nbytes):
    return int(min(VMEM_BYTES_V7X - (2 << 20), nbytes))


def _params(nbytes, ndims):
    return pltpu.CompilerParams(
        dimension_semantics=("arbitrary",) * ndims,
        vmem_limit_bytes=_vmem_limit(nbytes),
    )


def _rms(x, g):
    return x * lax.rsqrt(jnp.mean(x * x, axis=-1, keepdims=True) + EPS) * g


def _const_spec(shape):
    nd = len(shape)
    return pl.BlockSpec(shape, lambda *_: (0,) * nd, pipeline_mode=pl.Buffered(1))


def _ffn_kernel(x_ref, g_ref, wg_ref, wu_ref, wd_ref, gf_ref, o_ref, xn_ref, *, final_norm):
    k = pl.program_id(1)

    @pl.when(k == 0)
    def _():
        x = x_ref[...]
        xn_ref[...] = _rms(x, g_ref[...]).astype(BF16)
        o_ref[...] = x

    xn = xn_ref[...]
    gate = jnp.dot(xn, wg_ref[...], preferred_element_type=F32)
    up = jnp.dot(xn, wu_ref[...], preferred_element_type=F32)
    act = (jax.nn.silu(gate) * up * 0.5).astype(BF16)
    o_ref[...] += jnp.dot(act, wd_ref[...], preferred_element_type=F32)

    if final_norm:
        @pl.when(k == pl.num_programs(1) - 1)
        def _():
            o_ref[...] = _rms(o_ref[...], gf_ref[...])


def _ffn(x, g, wg, wu, wd, gf, *, final_norm, tm, tf):
    n, d = x.shape
    dff = wg.shape[1]
    vmem = (2 * tm * d * 4) * 2 + tm * d * 2 + 2 * 3 * d * tf * 2 + 3 * tm * tf * 4 + (4 << 20)
    return pl.pallas_call(
        functools.partial(_ffn_kernel, final_norm=final_norm),
        grid=(n // tm, dff // tf),
        in_specs=[
            pl.BlockSpec((tm, d), lambda i, k: (i, 0)),
            _const_spec((1, d)),
            pl.BlockSpec((d, tf), lambda i, k: (0, k)),
            pl.BlockSpec((d, tf), lambda i, k: (0, k)),
            pl.BlockSpec((tf, d), lambda i, k: (k, 0)),
            _const_spec((1, d)),
        ],
        out_specs=pl.BlockSpec((tm, d), lambda i, k: (i, 0)),
        out_shape=jax.ShapeDtypeStruct((n, d), F32),
        scratch_shapes=[pltpu.VMEM((tm, d), BF16)],
        compiler_params=_params(vmem, 2),
        name="ffn",
    )(x, g, wg, wu, wd, gf)


def _norm_proj_kernel(x_ref, g_ref, w_ref, o_ref):
    xn = _rms(x_ref[...], g_ref[...]).astype(BF16)
    o_ref[...] = jnp.dot(xn, w_ref[...], preferred_element_type=F32)


def _norm_proj(x, g, w, *, tm):
    n, d = x.shape
    dout = w.shape[1]
    vmem = 2 * tm * d * 4 + 2 * tm * dout * 4 + d * dout * 2 + tm * d * 6 + (4 << 20)
    return pl.pallas_call(
        _norm_proj_kernel,
        grid=(n // tm,),
        in_specs=[
            pl.BlockSpec((tm, d), lambda i: (i, 0)),
            _const_spec((1, d)),
            _const_spec((d, dout)),
        ],
        out_specs=pl.BlockSpec((tm, dout), lambda i: (i, 0)),
        out_shape=jax.ShapeDtypeStruct((n, dout), F32),
        compiler_params=_params(vmem, 1),
        name="in_proj",
    )(x, g, w)


def _s5_prep_kernel(lr_ref, li_ref, ldt_ref, br_ref, bi_ref, ar_ref, ai_ref, bbr_ref, bbi_ref):
    lr = lr_ref[...]
    li = li_ref[...]
    dt = jnp.exp(ldt_ref[...])
    mag = jnp.exp(lr * dt)
    abar_re = mag * jnp.cos(li * dt)
    abar_im = mag * jnp.sin(li * dt)
    nr, ni = abar_re - 1.0, abar_im
    den = lr * lr + li * li
    fr = (nr * lr + ni * li) / den
    fi = (ni * lr - nr * li) / den
    ar_ref[...] = abar_re
    ai_ref[...] = abar_im
    for h in range(br_ref.shape[0]):
        br = br_ref[h]
        bi = bi_ref[h]
        bbr_ref[h] = fr * br - fi * bi
        bbi_ref[h] = fr * bi + fi * br


def _s5_prep(a_re, a_im, log_dt, b_re_hgp, b_im_hgp):
    g, p = a_re.shape
    h = b_re_hgp.shape[0]
    return pl.pallas_call(
        _s5_prep_kernel,
        out_shape=(
            jax.ShapeDtypeStruct((g, p), F32),
            jax.ShapeDtypeStruct((g, p), F32),
            jax.ShapeDtypeStruct((h, g, p), F32),
            jax.ShapeDtypeStruct((h, g, p), F32),
        ),
        name="s5_prep",
    )(a_re, a_im, log_dt.reshape(g, 1), b_re_hgp, b_im_hgp)


def _s5_kernel(u_ref, rb_ref, rc_ref, at_ref, d_ref, wglu_ref, bglu_ref, g_ref, o_ref,
               ul_ref, s_ref, y_ref, yn_ref, st_ref, *, tc, nb, npair):
    c = pl.program_id(0)
    half = s_ref.shape[1] // 2
    rows_per_t = 2 * nb

    @pl.when(c == 0)
    def _():
        ul_ref[...] = jnp.zeros(ul_ref.shape, F32)
        st_ref[...] = jnp.zeros(st_ref.shape, F32)

    for jj in range(npair):
        for j2 in range(2):
            j = 2 * jj + j2
            for b in range(nb):
                ul_ref[jj, j2, pl.ds(b * 2 + j2, tc, stride=rows_per_t), :] = (
                    u_ref[b, :, j * LANES:(j + 1) * LANES])

    for jj in range(npair):
        lhs = jnp.concatenate([ul_ref[jj, 0], ul_ref[jj, 1]], axis=-1).astype(BF16)
        s_ref[...] = jnp.dot(lhs, rb_ref[jj], preferred_element_type=F32)
        a_r = at_ref[jj, :, :half]
        a_i = at_ref[jj, :, half:]

        def step(t, carry, a_r=a_r, a_i=a_i):
            s_r, s_i = carry
            r0 = pl.multiple_of(t * rows_per_t, rows_per_t)
            n_r = a_r * s_r - a_i * s_i + s_ref[pl.ds(r0, rows_per_t), :half]
            n_i = a_r * s_i + a_i * s_r + s_ref[pl.ds(r0, rows_per_t), half:]
            s_ref[pl.ds(r0, rows_per_t), :half] = n_r
            s_ref[pl.ds(r0, rows_per_t), half:] = n_i
            return n_r, n_i

        s_r, s_i = lax.fori_loop(0, tc, step, (st_ref[jj, :, :half], st_ref[jj, :, half:]), unroll=8)
        st_ref[jj, :, :half] = s_r
        st_ref[jj, :, half:] = s_i

        yy = jnp.dot(s_ref[...].astype(BF16), rc_ref[jj], preferred_element_type=F32)
        y_ref[0] = yy[:, :LANES]
        y_ref[1] = yy[:, LANES:]
        for j2 in range(2):
            j = 2 * jj + j2
            for b in range(nb):
                yn_ref[b * tc:(b + 1) * tc, j * LANES:(j + 1) * LANES] = (
                    y_ref[j2, pl.ds(b * 2 + j2, tc, stride=rows_per_t), :])

    w = u_ref.shape[-1]
    u = u_ref[...].reshape(nb * tc, w)
    y = yn_ref[...] + d_ref[...] * u
    y = jax.nn.gelu(y)
    z = jnp.dot(y.astype(BF16), wglu_ref[...], preferred_element_type=F32) + bglu_ref[...]
    y = y * jax.nn.sigmoid(z)
    o_ref[...] = _rms(y, g_ref[...]).astype(BF16).reshape(nb, tc, w)


def _s5(u3, rb, rc, at, d, wglu, bglu, g, *, tc):
    nb, L, _ = u3.shape
    w = wglu.shape[0]
    npair = rb.shape[0]
    nstate = rb.shape[2]
    rows = tc * 2 * nb
    scratch = [
        pltpu.VMEM((npair, 2, rows, LANES), F32),
        pltpu.VMEM((rows, nstate), F32),
        pltpu.VMEM((2, rows, LANES), F32),
        pltpu.VMEM((nb * tc, w), F32),
        pltpu.VMEM((npair, 2 * nb, nstate), F32),
    ]
    vmem = (npair * 2 * rows * LANES * 4 + rows * nstate * 4 + 2 * rows * LANES * 4 + nb * tc * w * 4
            + 2 * nb * tc * w * 4 + 2 * nb * tc * w * 2 + rb.size * 2 + rc.size * 2 + w * w * 2
            + rows * nstate * 4 + 4 * nb * tc * w * 4 + (4 << 20))
    return pl.pallas_call(
        functools.partial(_s5_kernel, tc=tc, nb=nb, npair=npair),
        grid=(L // tc,),
        in_specs=[
            pl.BlockSpec((nb, tc, w), lambda c: (0, c, 0)),
            _const_spec(rb.shape),
            _const_spec(rc.shape),
            _const_spec(at.shape),
            _const_spec((1, w)),
            _const_spec((w, w)),
            _const_spec((1, w)),
            _const_spec((1, w)),
        ],
        out_specs=pl.BlockSpec((nb, tc, w), lambda c: (0, c, 0)),
        out_shape=jax.ShapeDtypeStruct((nb, L, w), BF16),
        scratch_shapes=scratch,
        compiler_params=_params(vmem, 1),
        name="s5_mixer",
    )(u3, rb, rc, at, d, wglu, bglu, g)


def _pool_kernel(v_ref, wp_ref, sc_ref, g_ref, o_ref, ext_ref, *, tc, nb):
    c = pl.program_id(0)
    w = v_ref.shape[-1]
    gw = w // len(POOL_WINDOWS)

    @pl.when(c == 0)
    def _():
        ext_ref[:, :POOL_HALO, :] = jnp.zeros((nb, POOL_HALO, w), F32)

    ext_ref[:, POOL_HALO:, :] = v_ref[...]
    t_idx = c * tc + lax.broadcasted_iota(jnp.int32, (tc, 1), 0)

    for b in range(nb):
        zs = []
        ssq = jnp.zeros((tc, 1), F32)
        for gi, win in enumerate(POOL_WINDOWS):
            e = ext_ref[b, :, gi * gw:(gi + 1) * gw]
            s = e
            shift = 1
            while shift < win:
                s = s + pltpu.roll(s, shift, axis=0)
                shift *= 2
            cnt = jnp.minimum(t_idx + 1, win).astype(F32)
            pooled = s[POOL_HALO:] / cnt - e[POOL_HALO:]
            z = jnp.dot(pooled.astype(BF16), wp_ref[gi], preferred_element_type=F32)
            z = z * sc_ref[:, gi * gw:(gi + 1) * gw]
            ssq = ssq + jnp.sum(z * z, axis=-1, keepdims=True)
            zs.append(z)
        inv = lax.rsqrt(ssq / w + EPS)
        for gi in range(len(POOL_WINDOWS)):
            o_ref[b, :, gi * gw:(gi + 1) * gw] = (
                zs[gi] * inv * g_ref[:, gi * gw:(gi + 1) * gw]).astype(BF16)

    ext_ref[:, :POOL_HALO, :] = ext_ref[:, tc:, :]


def _pool(u3, wp, sc, g, *, tc):
    nb, L, d2 = u3.shape
    w = d2 // 2
    vmem = (nb * (tc + POOL_HALO) * w * 4 + 2 * nb * tc * w * 4 + 2 * nb * tc * w * 2 + wp.size * 2
            + 8 * tc * w * 4 + (4 << 20))
    return pl.pallas_call(
        functools.partial(_pool_kernel, tc=tc, nb=nb),
        grid=(L // tc,),
        in_specs=[
            pl.BlockSpec((nb, tc, w), lambda c: (0, c, 1)),
            _const_spec(wp.shape),
            _const_spec((1, w)),
            _const_spec((1, w)),
        ],
        out_specs=pl.BlockSpec((nb, tc, w), lambda c: (0, c, 0)),
        out_shape=jax.ShapeDtypeStruct((nb, L, w), BF16),
        scratch_shapes=[pltpu.VMEM((nb, tc + POOL_HALO, w), F32)],
        compiler_params=_params(vmem, 1),
        name="pool_mixer",
    )(u3, wp, sc, g)


def _merge_proj_kernel(h_ref, a_ref, b_ref, wa_ref, wb_ref, o_ref):
    acc = jnp.dot(a_ref[...], wa_ref[...], preferred_element_type=F32)
    acc = acc + jnp.dot(b_ref[...], wb_ref[...], preferred_element_type=F32)
    o_ref[...] = h_ref[...] + acc


def _merge_proj(h, a, b, wa, wb, *, tm):
    n, d = h.shape
    w = a.shape[1]
    vmem = 4 * tm * d * 4 + 4 * tm * w * 2 + 2 * w * d * 2 + tm * d * 4 + (4 << 20)
    return pl.pallas_call(
        _merge_proj_kernel,
        grid=(n // tm,),
        in_specs=[
            pl.BlockSpec((tm, d), lambda i: (i, 0)),
            pl.BlockSpec((tm, w), lambda i: (i, 0)),
            pl.BlockSpec((tm, w), lambda i: (i, 0)),
            _const_spec((w, d)),
            _const_spec((w, d)),
        ],
        out_specs=pl.BlockSpec((tm, d), lambda i: (i, 0)),
        out_shape=jax.ShapeDtypeStruct((n, d), F32),
        compiler_params=_params(vmem, 1),
        name="merge_proj",
    )(h, a, b, wa, wb)


def _kv_kernel(m_ref, g_ref, wk_ref, wv_ref, k_ref, v_ref, mn_ref):
    @pl.when(pl.program_id(0) == 0)
    def _():
        mn_ref[...] = _rms(m_ref[...], g_ref[...]).astype(BF16)

    mn = mn_ref[...]
    k_ref[...] = jnp.dot(mn, wk_ref[...], preferred_element_type=F32).astype(BF16)
    v_ref[...] = jnp.dot(mn, wv_ref[...], preferred_element_type=F32).astype(BF16)


def _kv(mem2, g, wk, wv, *, tn):
    n, d = mem2.shape
    vmem = n * d * 4 + n * d * 2 + 4 * d * tn * 2 + 4 * n * tn * 2 + 2 * n * tn * 4 + n * d * 4 + (4 << 20)
    return pl.pallas_call(
        _kv_kernel,
        grid=(d // tn,),
        in_specs=[
            _const_spec((n, d)),
            _const_spec((1, d)),
            pl.BlockSpec((d, tn), lambda j: (0, j)),
            pl.BlockSpec((d, tn), lambda j: (0, j)),
        ],
        out_specs=(pl.BlockSpec((n, tn), lambda j: (0, j)), pl.BlockSpec((n, tn), lambda j: (0, j))),
        out_shape=(jax.ShapeDtypeStruct((n, d), BF16), jax.ShapeDtypeStruct((n, d), BF16)),
        scratch_shapes=[pltpu.VMEM((n, d), BF16)],
        compiler_params=_params(vmem, 1),
        name="mem_kv",
    )(mem2, g, wk, wv)


def _xattn_kernel(h_ref, g_ref, wq_ref, k_ref, v_ref, wo_ref, o_ref):
    h = h_ref[...]
    d = h.shape[-1]
    hd = d // MEM_HEADS
    q = jnp.dot(_rms(h, g_ref[...]).astype(BF16), wq_ref[...], preferred_element_type=F32)
    outs = []
    for hh in range(MEM_HEADS):
        qh = q[:, hh * hd:(hh + 1) * hd].astype(BF16)
        kh = k_ref[:, hh * hd:(hh + 1) * hd]
        s = lax.dot_general(qh, kh, (((1,), (1,)), ((), ())), preferred_element_type=F32)
        s = s * (hd ** -0.5)
        e = jnp.exp(s - jnp.max(s, axis=-1, keepdims=True))
        p = e / jnp.sum(e, axis=-1, keepdims=True)
        outs.append(jnp.dot(p.astype(BF16), v_ref[:, hh * hd:(hh + 1) * hd],
                            preferred_element_type=F32).astype(BF16))
    o = jnp.concatenate(outs, axis=-1)
    o_ref[...] = h + jnp.dot(o, wo_ref[...], preferred_element_type=F32)


def _xattn(h, g, wq, k, v, wo, *, nb, tm):
    n, d = h.shape
    rows_b = n // nb
    nm = k.shape[0] // nb
    tiles = rows_b // tm
    vmem = 4 * tm * d * 4 + 2 * d * d * 2 + 4 * nm * d * 2 + 4 * tm * d * 4 + (4 << 20)
    return pl.pallas_call(
        _xattn_kernel,
        grid=(nb, tiles),
        in_specs=[
            pl.BlockSpec((tm, d), lambda b, i: (b * tiles + i, 0)),
            _const_spec((1, d)),
            _const_spec((d, d)),
            pl.BlockSpec((nm, d), lambda b, i: (b, 0)),
            pl.BlockSpec((nm, d), lambda b, i: (b, 0)),
            _const_spec((d, d)),
        ],
        out_specs=pl.BlockSpec((tm, d), lambda b, i: (b * tiles + i, 0)),
        out_shape=jax.ShapeDtypeStruct((n, d), F32),
        compiler_params=_params(vmem, 2),
        name="xattn",
    )(h, g, wq, k, v, wo)


def _s5_operands(abar_re, abar_im, bbar_re_hgp, bbar_im_hgp, c_re, c_im, nb):
    g, p = abar_re.shape
    h = bbar_re_hgp.shape[0]
    gl = LANES // h
    npair = g // (2 * gl)
    eye = jnp.eye(gl, dtype=F32)
    bb = jnp.stack([bbar_re_hgp, bbar_im_hgp]).reshape(2, h, npair, 2, gl, p)
    rb = jnp.einsum('rhjkgp,gq->jkghrqp', bb, eye).reshape(npair, 2 * LANES, 2 * gl * p)
    cc = jnp.stack([c_re, -c_im]).reshape(2, npair, 2, gl, h, p)
    rc = jnp.einsum('rjkqhp,gq->jrgpkqh', cc, eye).reshape(npair, 2 * gl * p, 2 * LANES)
    ab = jnp.stack([abar_re, abar_im]).reshape(2, npair, 2, gl, p)
    at = jnp.transpose(ab, (1, 2, 0, 3, 4)).reshape(npair, 1, 2, 2 * gl * p)
    at = jnp.broadcast_to(at, (npair, nb, 2, 2 * gl * p)).reshape(npair, 2 * nb, 2 * gl * p)
    return rb.astype(BF16), rc.astype(BF16), at


def kernel(x, mem, g_ffn1, w1_gate, w1_up, w1_down, g_mix, w_in, ssm_a_re, ssm_a_im, ssm_log_dt,
           ssm_b_re, ssm_b_im, ssm_c_re, ssm_c_im, ssm_d, w_glu, b_glu, w_pool, pool_scale,
           g_out_ssm, g_out_pool, w_out, g_xattn, g_mem, w_q, w_k, w_v, w_o,
           g_ffn2, w2_gate, w2_up, w2_down, g_final):
    nb, L, d = x.shape
    n = nb * L
    depth = g_ffn1.shape[0]
    w_ssm = ssm_d.shape[1]
    bf = lambda a: a.astype(BF16)
    row = lambda a: a.reshape(1, -1)

    h = x.reshape(n, d)
    for l in range(depth):
        last = l == depth - 1
        h = _ffn(h, row(g_ffn1[l]), bf(w1_gate[l]), bf(w1_up[l]), bf(w1_down[l]), row(g_final),
                 final_norm=False, tm=1024, tf=512)

        u = _norm_proj(h, row(g_mix[l]), bf(w_in[l]), tm=1024)
        u3 = u.reshape(nb, L, -1)

        abar_re, abar_im, bbar_re, bbar_im = _s5_prep(
            ssm_a_re[l], ssm_a_im[l], ssm_log_dt[l],
            jnp.transpose(ssm_b_re[l], (2, 0, 1)), jnp.transpose(ssm_b_im[l], (2, 0, 1)))
        rb, rc, at = _s5_operands(abar_re, abar_im, bbar_re, bbar_im, ssm_c_re[l], ssm_c_im[l], nb)
        m_ssm = _s5(u3, rb, rc, at, row(ssm_d[l]), bf(w_glu[l]), row(b_glu[l]), row(g_out_ssm[l]),
                    tc=256)
        m_pool = _pool(u3, bf(w_pool[l]), row(pool_scale[l]), row(g_out_pool[l]), tc=256)

        wo_b = bf(w_out[l])
        h = _merge_proj(h, m_ssm.reshape(n, w_ssm), m_pool.reshape(n, -1),
                        wo_b[:w_ssm], wo_b[w_ssm:], tm=1024)

        k, v = _kv(mem.reshape(nb * mem.shape[1], d), row(g_mem[l]), bf(w_k[l]), bf(w_v[l]), tn=512)
        h = _xattn(h, row(g_xattn[l]), bf(w_q[l]), k, v, bf(w_o[l]), nb=nb, tm=512)

        h = _ffn(h, row(g_ffn2[l]), bf(w2_gate[l]), bf(w2_up[l]), bf(w2_down[l]), row(g_final),
                 final_norm=last, tm=1024, tf=512)
    return h.reshape(nb, L, d)
```

```python
import functools

import jax
import jax.numpy as jnp
from jax import lax
from jax.experimental import pallas as pl
from jax.experimental.pallas import tpu as pltpu

F32 = jnp.float32
BF16 = jnp.bfloat16

EPS = 1e-6
POOL_WINDOWS = (2, 4, 8, 16)
MEM_HEADS = 4

LANES = 128
SUBLANES = 8
VMEM_BYTES_V7X = 64 * 1024 * 1024
VMEM_RESERVE = 2 * 1024 * 1024
POOL_HALO = 16

TILES = dict(ffn=(1024, 512), mixer=128, proj=1024, xattn=512, kv=512)


def _params(block_bytes, temp_bytes, ndims):
    return pltpu.CompilerParams(
        dimension_semantics=("arbitrary",) * ndims,
        vmem_limit_bytes=int(min(VMEM_BYTES_V7X - VMEM_RESERVE, block_bytes + temp_bytes)),
    )


def _rms(x, g):
    return x * lax.rsqrt(jnp.mean(x * x, axis=-1, keepdims=True) + EPS) * g


def _const_spec(shape):
    nd = len(shape)
    return pl.BlockSpec(shape, lambda *_: (0,) * nd, pipeline_mode=pl.Buffered(1))


def _ffn_kernel(x_ref, g_ref, wg_ref, wu_ref, wd_ref, gf_ref, o_ref, xn_ref, *, final_norm):
    k = pl.program_id(1)

    @pl.when(k == 0)
    def _():
        x = x_ref[...]
        xn_ref[...] = _rms(x, g_ref[...]).astype(BF16)
        o_ref[...] = x

    xn = xn_ref[...]
    gate = jnp.dot(xn, wg_ref[...], preferred_element_type=F32)
    up = jnp.dot(xn, wu_ref[...], preferred_element_type=F32)
    act = (jax.nn.silu(gate) * up * 0.5).astype(BF16)
    o_ref[...] += jnp.dot(act, wd_ref[...], preferred_element_type=F32)

    if final_norm:
        @pl.when(k == pl.num_programs(1) - 1)
        def _():
            o_ref[...] = _rms(o_ref[...], gf_ref[...])


def _ffn(x, g, wg, wu, wd, gf, *, final_norm):
    tm, tf = TILES["ffn"]
    n, d = x.shape
    dff = wg.shape[1]
    blocks = 2 * (tm * d * 4) * 2 + tm * d * 2 + 2 * 3 * d * tf * 2
    temps = 3 * tm * tf * 4 + tm * d * 4
    return pl.pallas_call(
        functools.partial(_ffn_kernel, final_norm=final_norm),
        grid=(n // tm, dff // tf),
        in_specs=[
            pl.BlockSpec((tm, d), lambda i, k: (i, 0)),
            _const_spec((1, d)),
            pl.BlockSpec((d, tf), lambda i, k: (0, k)),
            pl.BlockSpec((d, tf), lambda i, k: (0, k)),
            pl.BlockSpec((tf, d), lambda i, k: (k, 0)),
            _const_spec((1, d)),
        ],
        out_specs=pl.BlockSpec((tm, d), lambda i, k: (i, 0)),
        out_shape=jax.ShapeDtypeStruct((n, d), F32),
        scratch_shapes=[pltpu.VMEM((tm, d), BF16)],
        compiler_params=_params(blocks, temps, 2),
        name="ffn",
    )(x, g, wg, wu, wd, gf)


def _s5_prep_kernel(lr_ref, li_ref, ldt_ref, br_ref, bi_ref, ar_ref, ai_ref, bbr_ref, bbi_ref):
    lr = lr_ref[...]
    li = li_ref[...]
    dt = jnp.exp(ldt_ref[...])
    mag = jnp.exp(lr * dt)
    abar_re = mag * jnp.cos(li * dt)
    abar_im = mag * jnp.sin(li * dt)
    nr, ni = abar_re - 1.0, abar_im
    den = lr * lr + li * li
    fr = (nr * lr + ni * li) / den
    fi = (ni * lr - nr * li) / den
    ar_ref[...] = abar_re
    ai_ref[...] = abar_im
    for h in range(br_ref.shape[0]):
        br = br_ref[h]
        bi = bi_ref[h]
        bbr_ref[h] = fr * br - fi * bi
        bbi_ref[h] = fr * bi + fi * br


def _s5_prep(a_re, a_im, log_dt, b_re_hgp, b_im_hgp):
    g, p = a_re.shape
    h = b_re_hgp.shape[0]
    return pl.pallas_call(
        _s5_prep_kernel,
        out_shape=(
            jax.ShapeDtypeStruct((g, p), F32),
            jax.ShapeDtypeStruct((g, p), F32),
            jax.ShapeDtypeStruct((h, g, p), F32),
            jax.ShapeDtypeStruct((h, g, p), F32),
        ),
        name="s5_prep",
    )(a_re, a_im, log_dt.reshape(g, 1), b_re_hgp, b_im_hgp)


def _mixer_kernel(h0_ref, hn_ref, gmix_ref, win_ref, rb_ref, rc_ref, at_ref, d_ref, wglu_ref, bglu_ref,
                  gs_ref, wp_ref, sc_ref, gp_ref, o_ref,
                  u_ref, ul_ref, s_ref, y_ref, yn_ref, st_ref, ext_ref, du_ref, xn_ref, *,
                  tc, nb, npair):
    c = pl.program_id(0)
    d = hn_ref.shape[-1]
    w = d // 2
    half = s_ref.shape[-1] // 2
    rows_per_t = 2 * nb
    nbuf = s_ref.shape[0]

    def in_proj(h_ref):
        h = h_ref[...].reshape(nb * tc, d)
        return jnp.dot(_rms(h, gmix_ref[...]).astype(BF16), win_ref[...], preferred_element_type=F32)

    @pl.when(c == 0)
    def _():
        ul_ref[...] = jnp.zeros(ul_ref.shape, F32)
        st_ref[...] = jnp.zeros(st_ref.shape, F32)
        ext_ref[:, :POOL_HALO, :] = jnp.zeros((nb, POOL_HALO, w), F32)
        u_ref[...] = in_proj(h0_ref)

    for jj in range(npair):
        for j2 in range(2):
            j = 2 * jj + j2
            for b in range(nb):
                ul_ref[jj, j2, pl.ds(b * 2 + j2, tc, stride=rows_per_t), :] = (
                    u_ref[b * tc:(b + 1) * tc, j * LANES:(j + 1) * LANES])
    du_ref[...] = d_ref[...] * u_ref[:, :w]
    for b in range(nb):
        ext_ref[b, POOL_HALO:, :] = u_ref[b * tc:(b + 1) * tc, w:]

    xn_ref[...] = _rms(hn_ref[...].reshape(nb * tc, d), gmix_ref[...]).astype(BF16)
    pw = d // npair
    gw = w // len(POOL_WINDOWS)
    t_idx = c * tc + lax.broadcasted_iota(jnp.int32, (tc, 1), 0)

    def next_u(p):
        u_ref[:, p * pw:(p + 1) * pw] = jnp.dot(
            xn_ref[...], win_ref[:, p * pw:(p + 1) * pw], preferred_element_type=F32)

    def b_proj(jj):
        lhs = jnp.concatenate([ul_ref[jj, 0], ul_ref[jj, 1]], axis=-1).astype(BF16)
        s_ref[jj % nbuf] = jnp.dot(lhs, rb_ref[jj], preferred_element_type=F32)

    def scan(jj):
        sb = jj % nbuf
        a_r = at_ref[jj, :, :half]
        a_i = at_ref[jj, :, half:]
        s_r = st_ref[jj, :, :half]
        s_i = st_ref[jj, :, half:]
        for t in range(tc):
            r0 = t * rows_per_t
            n_r = a_r * s_r - a_i * s_i + s_ref[sb, r0:r0 + rows_per_t, :half]
            n_i = a_r * s_i + a_i * s_r + s_ref[sb, r0:r0 + rows_per_t, half:]
            s_ref[sb, r0:r0 + rows_per_t, :half] = n_r
            s_ref[sb, r0:r0 + rows_per_t, half:] = n_i
            s_r, s_i = n_r, n_i
        st_ref[jj, :, :half] = s_r
        st_ref[jj, :, half:] = s_i

    def c_proj(jj):
        sb = jj % nbuf
        yy = jnp.dot(s_ref[sb].astype(BF16), rc_ref[jj], preferred_element_type=F32)
        y_ref[sb, 0] = yy[:, :LANES]
        y_ref[sb, 1] = yy[:, LANES:]
        for j2 in range(2):
            j = 2 * jj + j2
            for b in range(nb):
                yn_ref[b * tc:(b + 1) * tc, j * LANES:(j + 1) * LANES] = (
                    y_ref[sb, j2, pl.ds(b * 2 + j2, tc, stride=rows_per_t), :])

    def s5_out():
        y = yn_ref[...] + du_ref[...]
        y = jax.nn.gelu(y)
        z = jnp.dot(y.astype(BF16), wglu_ref[...], preferred_element_type=F32) + bglu_ref[...]
        y = y * jax.nn.sigmoid(z)
        o_ref[:, :, :w] = _rms(y, gs_ref[...]).astype(BF16).reshape(nb, tc, w)

    def pool(b):
        zs = []
        ssq = jnp.zeros((tc, 1), F32)
        for gi, win in enumerate(POOL_WINDOWS):
            e = ext_ref[b, :, gi * gw:(gi + 1) * gw]
            s = e
            shift = 1
            while shift < win:
                s = s + pltpu.roll(s, shift, axis=0)
                shift *= 2
            cnt = jnp.minimum(t_idx + 1, win).astype(F32)
            pooled = s[POOL_HALO:] / cnt - e[POOL_HALO:]
            z = jnp.dot(pooled.astype(BF16), wp_ref[gi], preferred_element_type=F32)
            z = z * sc_ref[:, gi * gw:(gi + 1) * gw]
            ssq = ssq + jnp.sum(z * z, axis=-1, keepdims=True)
            zs.append(z)
        inv = lax.rsqrt(ssq / w + EPS)
        for gi in range(len(POOL_WINDOWS)):
            o_ref[b, :, w + gi * gw:w + (gi + 1) * gw] = (
                zs[gi] * inv * gp_ref[:, gi * gw:(gi + 1) * gw]).astype(BF16)
        ext_ref[b, :POOL_HALO, :] = ext_ref[b, tc:, :]

    for jj in range(min(nbuf, npair)):
        b_proj(jj)
    for jj in range(npair):
        if jj < npair - 1:
            next_u(jj)
        scan(jj)
        c_proj(jj)
        if jj + nbuf < npair:
            b_proj(jj + nbuf)
        for b in range(jj * nb // npair, (jj + 1) * nb // npair):
            pool(b)
    next_u(npair - 1)
    s5_out()


def _mixer(h3, gmix, win, rb, rc, at, dskip, wglu, bglu, gs, wp, sc, gp):
    tc = TILES["mixer"]
    nb, L, d = h3.shape
    w = d // 2
    npair = rb.shape[0]
    nstate = rb.shape[2]
    rows = tc * 2 * nb
    nbuf = 2
    scratch = [
        pltpu.VMEM((nb * tc, d), F32),
        pltpu.VMEM((npair, 2, rows, LANES), F32),
        pltpu.VMEM((nbuf, rows, nstate), F32),
        pltpu.VMEM((nbuf, 2, rows, LANES), F32),
        pltpu.VMEM((nb * tc, w), F32),
        pltpu.VMEM((npair, 2 * nb, nstate), F32),
        pltpu.VMEM((nb, tc + POOL_HALO, w), F32),
        pltpu.VMEM((nb * tc, w), F32),
        pltpu.VMEM((nb * tc, d), BF16),
    ]
    blocks = (3 * nb * tc * d * 4 + 3 * nb * tc * d * 2 + d * d * 2 + rb.size * 2 + rc.size * 2
              + at.size * 4 + w * w * 2 + wp.size * 2
              + (nb * tc * d + npair * 2 * rows * LANES + nbuf * rows * nstate + nbuf * 2 * rows * LANES
                 + 2 * nb * tc * w + npair * 2 * nb * nstate + nb * (tc + POOL_HALO) * w) * 4)
    temps = nb * tc * d * 6 + rows * nstate * 6 + 4 * nb * tc * w * 4
    nchunks = L // tc
    return pl.pallas_call(
        functools.partial(_mixer_kernel, tc=tc, nb=nb, npair=npair),
        grid=(nchunks,),
        in_specs=[
            pl.BlockSpec((nb, tc, d), lambda c: (0, 0, 0), pipeline_mode=pl.Buffered(1)),
            pl.BlockSpec((nb, tc, d), lambda c: (0, jnp.minimum(c + 1, nchunks - 1), 0)),
            _const_spec((1, d)),
            _const_spec((d, d)),
            _const_spec(rb.shape),
            _const_spec(rc.shape),
            _const_spec(at.shape),
            _const_spec((1, w)),
            _const_spec((w, w)),
            _const_spec((1, w)),
            _const_spec((1, w)),
            _const_spec(wp.shape),
            _const_spec((1, w)),
            _const_spec((1, w)),
        ],
        out_specs=pl.BlockSpec((nb, tc, d), lambda c: (0, c, 0)),
        out_shape=jax.ShapeDtypeStruct((nb, L, d), BF16),
        scratch_shapes=scratch,
        compiler_params=_params(blocks, temps, 1),
        name="mixer",
    )(h3, h3, gmix, win, rb, rc, at, dskip, wglu, bglu, gs, wp, sc, gp)


def _merge_proj_kernel(h_ref, m_ref, w_ref, o_ref):
    o_ref[...] = h_ref[...] + jnp.dot(m_ref[...], w_ref[...], preferred_element_type=F32)


def _merge_proj(h, m, w):
    tm = TILES["proj"]
    n, d = h.shape
    blocks = 4 * tm * d * 4 + 2 * tm * d * 2 + d * d * 2
    temps = tm * d * 4
    return pl.pallas_call(
        _merge_proj_kernel,
        grid=(n // tm,),
        in_specs=[
            pl.BlockSpec((tm, d), lambda i: (i, 0)),
            pl.BlockSpec((tm, d), lambda i: (i, 0)),
            _const_spec((d, d)),
        ],
        out_specs=pl.BlockSpec((tm, d), lambda i: (i, 0)),
        out_shape=jax.ShapeDtypeStruct((n, d), F32),
        compiler_params=_params(blocks, temps, 1),
        name="merge_proj",
    )(h, m, w)


def _kv_kernel(m_ref, g_ref, wk_ref, wv_ref, k_ref, v_ref, mn_ref):
    @pl.when(pl.program_id(0) == 0)
    def _():
        mn_ref[...] = _rms(m_ref[...], g_ref[...]).astype(BF16)

    mn = mn_ref[...]
    k_ref[...] = jnp.dot(mn, wk_ref[...], preferred_element_type=F32).astype(BF16)
    v_ref[...] = jnp.dot(mn, wv_ref[...], preferred_element_type=F32).astype(BF16)


def _kv(mem2, g, wk, wv):
    tn = TILES["kv"]
    n, d = mem2.shape
    blocks = n * d * 4 + n * d * 2 + 4 * d * tn * 2 + 4 * n * tn * 2
    temps = 2 * n * tn * 4 + n * d * 4
    return pl.pallas_call(
        _kv_kernel,
        grid=(d // tn,),
        in_specs=[
            _const_spec((n, d)),
            _const_spec((1, d)),
            pl.BlockSpec((d, tn), lambda j: (0, j)),
            pl.BlockSpec((d, tn), lambda j: (0, j)),
        ],
        out_specs=(pl.BlockSpec((n, tn), lambda j: (0, j)), pl.BlockSpec((n, tn), lambda j: (0, j))),
        out_shape=(jax.ShapeDtypeStruct((n, d), BF16), jax.ShapeDtypeStruct((n, d), BF16)),
        scratch_shapes=[pltpu.VMEM((n, d), BF16)],
        compiler_params=_params(blocks, temps, 1),
        name="mem_kv",
    )(mem2, g, wk, wv)


def _xattn_kernel(h_ref, g_ref, wq_ref, k_ref, v_ref, wo_ref, o_ref):
    h = h_ref[...]
    d = h.shape[-1]
    hd = d // MEM_HEADS
    q = jnp.dot(_rms(h, g_ref[...]).astype(BF16), wq_ref[...], preferred_element_type=F32)
    outs = []
    for hh in range(MEM_HEADS):
        qh = q[:, hh * hd:(hh + 1) * hd].astype(BF16)
        kh = k_ref[:, hh * hd:(hh + 1) * hd]
        s = lax.dot_general(qh, kh, (((1,), (1,)), ((), ())), preferred_element_type=F32)
        s = s * (hd ** -0.5)
        e = jnp.exp(s - jnp.max(s, axis=-1, keepdims=True))
        p = e / jnp.sum(e, axis=-1, keepdims=True)
        outs.append(jnp.dot(p.astype(BF16), v_ref[:, hh * hd:(hh + 1) * hd],
                            preferred_element_type=F32).astype(BF16))
    o = jnp.concatenate(outs, axis=-1)
    o_ref[...] = h + jnp.dot(o, wo_ref[...], preferred_element_type=F32)


def _xattn(h, g, wq, k, v, wo, *, nb):
    tm = TILES["xattn"]
    n, d = h.shape
    nm = k.shape[0] // nb
    tiles = n // nb // tm
    blocks = 4 * tm * d * 4 + 2 * d * d * 2 + 4 * nm * d * 2
    temps = 4 * tm * d * 4
    return pl.pallas_call(
        _xattn_kernel,
        grid=(nb, tiles),
        in_specs=[
            pl.BlockSpec((tm, d), lambda b, i: (b * tiles + i, 0)),
            _const_spec((1, d)),
            _const_spec((d, d)),
            pl.BlockSpec((nm, d), lambda b, i: (b, 0)),
            pl.BlockSpec((nm, d), lambda b, i: (b, 0)),
            _const_spec((d, d)),
        ],
        out_specs=pl.BlockSpec((tm, d), lambda b, i: (b * tiles + i, 0)),
        out_shape=jax.ShapeDtypeStruct((n, d), F32),
        compiler_params=_params(blocks, temps, 2),
        name="xattn",
    )(h, g, wq, k, v, wo)


def _s5_operands(abar_re, abar_im, bbar_re_hgp, bbar_im_hgp, c_re, c_im, nb):
    g, p = abar_re.shape
    h = bbar_re_hgp.shape[0]
    gl = LANES // h
    npair = g // (2 * gl)
    eye = jnp.eye(gl, dtype=F32)
    bb = jnp.stack([bbar_re_hgp, bbar_im_hgp]).reshape(2, h, npair, 2, gl, p)
    rb = jnp.einsum('rhjkgp,gq->jkghrqp', bb, eye).reshape(npair, 2 * LANES, 2 * gl * p)
    cc = jnp.stack([c_re, -c_im]).reshape(2, npair, 2, gl, h, p)
    rc = jnp.einsum('rjkqhp,gq->jrgpkqh', cc, eye).reshape(npair, 2 * gl * p, 2 * LANES)
    ab = jnp.stack([abar_re, abar_im]).reshape(2, npair, 2, gl, p)
    at = jnp.transpose(ab, (1, 2, 0, 3, 4)).reshape(npair, 1, 2, 2 * gl * p)
    at = jnp.broadcast_to(at, (npair, nb, 2, 2 * gl * p)).reshape(npair, 2 * nb, 2 * gl * p)
    return rb.astype(BF16), rc.astype(BF16), at


def kernel(x, mem, g_ffn1, w1_gate, w1_up, w1_down, g_mix, w_in, ssm_a_re, ssm_a_im, ssm_log_dt,
           ssm_b_re, ssm_b_im, ssm_c_re, ssm_c_im, ssm_d, w_glu, b_glu, w_pool, pool_scale,
           g_out_ssm, g_out_pool, w_out, g_xattn, g_mem, w_q, w_k, w_v, w_o,
           g_ffn2, w2_gate, w2_up, w2_down, g_final):
    nb, L, d = x.shape
    n = nb * L
    depth = g_ffn1.shape[0]
    bf = lambda a: a.astype(BF16)
    row = lambda a: a.reshape(1, -1)

    h = x.reshape(n, d)
    for l in range(depth):
        last = l == depth - 1
        h = _ffn(h, row(g_ffn1[l]), bf(w1_gate[l]), bf(w1_up[l]), bf(w1_down[l]), row(g_final),
                 final_norm=False)

        abar_re, abar_im, bbar_re, bbar_im = _s5_prep(
            ssm_a_re[l], ssm_a_im[l], ssm_log_dt[l],
            jnp.transpose(ssm_b_re[l], (2, 0, 1)), jnp.transpose(ssm_b_im[l], (2, 0, 1)))
        rb, rc, at = _s5_operands(abar_re, abar_im, bbar_re, bbar_im, ssm_c_re[l], ssm_c_im[l], nb)
        merged = _mixer(h.reshape(nb, L, d), row(g_mix[l]), bf(w_in[l]), rb, rc, at, row(ssm_d[l]),
                        bf(w_glu[l]), row(b_glu[l]), row(g_out_ssm[l]),
                        bf(w_pool[l]), row(pool_scale[l]), row(g_out_pool[l]))
        h = _merge_proj(h, merged.reshape(n, d), bf(w_out[l]))

        k, v = _kv(mem.reshape(nb * mem.shape[1], d), row(g_mem[l]), bf(w_k[l]), bf(w_v[l]))
        h = _xattn(h, row(g_xattn[l]), bf(w_q[l]), k, v, bf(w_o[l]), nb=nb)

        h = _ffn(h, row(g_ffn2[l]), bf(w2_gate[l]), bf(w2_up[l]), bf(w2_down[l]), row(g_final),
                 final_norm=last)
    return h.reshape(nb, L, d)
```

```python
import functools

import jax
import jax.numpy as jnp
from jax import lax
from jax.experimental import pallas as pl
from jax.experimental.pallas import tpu as pltpu

F32 = jnp.float32
BF16 = jnp.bfloat16

EPS = 1e-6
POOL_WINDOWS = (2, 4, 8, 16)
MEM_HEADS = 4

LANES = 128
SUBLANES = 8
VMEM_BYTES_V7X = 64 * 1024 * 1024
VMEM_RESERVE = 2 * 1024 * 1024
POOL_HALO = 16

TILES = dict(ffn=(1024, 512), mixer=128, proj=1024, xattn=512, kv=512)


def _params(block_bytes, temp_bytes, ndims):
    return pltpu.CompilerParams(
        dimension_semantics=("arbitrary",) * ndims,
        vmem_limit_bytes=int(min(VMEM_BYTES_V7X - VMEM_RESERVE, block_bytes + temp_bytes)),
    )


def _rms(x, g):
    return x * lax.rsqrt(jnp.mean(x * x, axis=-1, keepdims=True) + EPS) * g


def _const_spec(shape):
    nd = len(shape)
    return pl.BlockSpec(shape, lambda *_: (0,) * nd, pipeline_mode=pl.Buffered(1))


def _ffn_kernel(*refs, final_norm, n_side):
    x_ref, g_ref, wg_ref, wu_ref, wd_ref, gf_ref = refs[:6]
    side_in = refs[6:6 + n_side]
    o_ref = refs[6 + n_side]
    side_out = refs[7 + n_side:7 + 2 * n_side]
    xn_ref = refs[7 + 2 * n_side]
    k = pl.program_id(1)

    @pl.when(k == 0)
    def _():
        x = x_ref[...]
        xn_ref[...] = _rms(x, g_ref[...]).astype(BF16)
        o_ref[...] = x

    xn = xn_ref[...]
    tf = wg_ref.shape[1]
    hf = tf // 2
    acts = []
    for p in range(2):
        gate = jnp.dot(xn, wg_ref[:, p * hf:(p + 1) * hf], preferred_element_type=F32)
        up = jnp.dot(xn, wu_ref[:, p * hf:(p + 1) * hf], preferred_element_type=F32)
        acts.append((jax.nn.silu(gate) * up * 0.5).astype(BF16))
    acc = jnp.dot(acts[0], wd_ref[:hf, :], preferred_element_type=F32)
    acc = acc + jnp.dot(acts[1], wd_ref[hf:, :], preferred_element_type=F32)
    o_ref[...] += acc

    for src, dst in zip(side_in, side_out):
        dst[...] = src[...].astype(BF16)

    if final_norm:
        @pl.when(k == pl.num_programs(1) - 1)
        def _():
            o_ref[...] = _rms(o_ref[...], gf_ref[...])


def _cast_spec(shape, ni, nk):
    r, c = shape
    row_align = 2 * SUBLANES
    if c % (ni * LANES) == 0 and r % (nk * row_align) == 0:
        return pl.BlockSpec((r // nk, c // ni), lambda i, k: (k, i))
    m = max(m for m in range(1, nk + 1) if c % (m * LANES) == 0)
    assert r % (ni * row_align) == 0, shape
    return pl.BlockSpec((r // ni, c // m), lambda i, k: (i, jnp.minimum(k, m - 1)))


def _ffn(x, g, wg, wu, wd, gf, *, final_norm, side=()):
    tm, tf = TILES["ffn"]
    n, d = x.shape
    dff = wg.shape[1]
    ni, nk = n // tm, dff // tf
    side_specs = [_cast_spec(w.shape, ni, nk) for w in side]
    side_bytes = sum(2 * sp.block_shape[0] * sp.block_shape[1] * (4 + 2) for sp in side_specs)
    blocks = 2 * (tm * d * 4) * 2 + tm * d * 2 + 2 * 3 * d * tf * 2 + side_bytes
    temps = 3 * tm * tf * 4 + tm * d * 4
    outs = pl.pallas_call(
        functools.partial(_ffn_kernel, final_norm=final_norm, n_side=len(side)),
        grid=(ni, nk),
        in_specs=[
            pl.BlockSpec((tm, d), lambda i, k: (i, 0)),
            _const_spec((1, d)),
            pl.BlockSpec((d, tf), lambda i, k: (0, k)),
            pl.BlockSpec((d, tf), lambda i, k: (0, k)),
            pl.BlockSpec((tf, d), lambda i, k: (k, 0)),
            _const_spec((1, d)),
        ] + side_specs,
        out_specs=[pl.BlockSpec((tm, d), lambda i, k: (i, 0))] + side_specs,
        out_shape=[jax.ShapeDtypeStruct((n, d), F32)]
        + [jax.ShapeDtypeStruct(w.shape, BF16) for w in side],
        scratch_shapes=[pltpu.VMEM((tm, d), BF16)],
        compiler_params=_params(blocks, temps, 2),
        name="ffn",
    )(x, g, wg, wu, wd, gf, *side)
    return outs[0], list(outs[1:])


def _s5_prep_kernel(lr_ref, li_ref, ldt_ref, br_ref, bi_ref, ar_ref, ai_ref, bbr_ref, bbi_ref):
    lr = lr_ref[...]
    li = li_ref[...]
    dt = jnp.exp(ldt_ref[...])
    mag = jnp.exp(lr * dt)
    abar_re = mag * jnp.cos(li * dt)
    abar_im = mag * jnp.sin(li * dt)
    nr, ni = abar_re - 1.0, abar_im
    den = lr * lr + li * li
    fr = (nr * lr + ni * li) / den
    fi = (ni * lr - nr * li) / den
    ar_ref[...] = abar_re
    ai_ref[...] = abar_im
    for h in range(br_ref.shape[0]):
        br = br_ref[h]
        bi = bi_ref[h]
        bbr_ref[h] = fr * br - fi * bi
        bbi_ref[h] = fr * bi + fi * br


def _s5_prep(a_re, a_im, log_dt, b_re_hgp, b_im_hgp):
    g, p = a_re.shape
    h = b_re_hgp.shape[0]
    return pl.pallas_call(
        _s5_prep_kernel,
        out_shape=(
            jax.ShapeDtypeStruct((g, p), F32),
            jax.ShapeDtypeStruct((g, p), F32),
            jax.ShapeDtypeStruct((h, g, p), F32),
            jax.ShapeDtypeStruct((h, g, p), F32),
        ),
        name="s5_prep",
    )(a_re, a_im, log_dt.reshape(g, 1), b_re_hgp, b_im_hgp)


def _mixer_kernel(h0_ref, hn_ref, gmix_ref, win_ref, rb_ref, rc_ref, at_ref, d_ref, wglu_ref, bglu_ref,
                  gs_ref, wp_ref, sc_ref, gp_ref, o_ref,
                  u_ref, ul_ref, s_ref, y_ref, yn_ref, st_ref, ext_ref, du_ref, xn_ref, *,
                  tc, nb, npair):
    c = pl.program_id(0)
    d = hn_ref.shape[-1]
    w = d // 2
    half = s_ref.shape[-1] // 2
    rows_per_t = 2 * nb
    nbuf = s_ref.shape[0]

    def in_proj(h_ref):
        h = h_ref[...].reshape(nb * tc, d)
        return jnp.dot(_rms(h, gmix_ref[...]).astype(BF16), win_ref[...], preferred_element_type=F32)

    @pl.when(c == 0)
    def _():
        ul_ref[...] = jnp.zeros(ul_ref.shape, F32)
        st_ref[...] = jnp.zeros(st_ref.shape, F32)
        ext_ref[:, :POOL_HALO, :] = jnp.zeros((nb, POOL_HALO, w), F32)
        u_ref[...] = in_proj(h0_ref)

    for jj in range(npair):
        for j2 in range(2):
            j = 2 * jj + j2
            for b in range(nb):
                ul_ref[jj, j2, pl.ds(b * 2 + j2, tc, stride=rows_per_t), :] = (
                    u_ref[b * tc:(b + 1) * tc, j * LANES:(j + 1) * LANES])
    du_ref[...] = d_ref[...] * u_ref[:, :w]
    for b in range(nb):
        ext_ref[b, POOL_HALO:, :] = u_ref[b * tc:(b + 1) * tc, w:]

    xn_ref[...] = _rms(hn_ref[...].reshape(nb * tc, d), gmix_ref[...]).astype(BF16)
    pw = d // npair
    gw = w // len(POOL_WINDOWS)
    t_idx = c * tc + lax.broadcasted_iota(jnp.int32, (tc, 1), 0)

    def next_u(p):
        u_ref[:, p * pw:(p + 1) * pw] = jnp.dot(
            xn_ref[...], win_ref[:, p * pw:(p + 1) * pw], preferred_element_type=F32)

    def b_proj(jj):
        lhs = jnp.concatenate([ul_ref[jj, 0], ul_ref[jj, 1]], axis=-1).astype(BF16)
        s_ref[jj % nbuf] = jnp.dot(lhs, rb_ref[jj], preferred_element_type=F32)

    def scan(jj):
        sb = jj % nbuf
        a_r = at_ref[jj, :, :half]
        a_i = at_ref[jj, :, half:]
        s_r = st_ref[jj, :, :half]
        s_i = st_ref[jj, :, half:]
        for t in range(tc):
            r0 = t * rows_per_t
            n_r = a_r * s_r - a_i * s_i + s_ref[sb, r0:r0 + rows_per_t, :half]
            n_i = a_r * s_i + a_i * s_r + s_ref[sb, r0:r0 + rows_per_t, half:]
            s_ref[sb, r0:r0 + rows_per_t, :half] = n_r
            s_ref[sb, r0:r0 + rows_per_t, half:] = n_i
            s_r, s_i = n_r, n_i
        st_ref[jj, :, :half] = s_r
        st_ref[jj, :, half:] = s_i

    def c_proj(jj):
        sb = jj % nbuf
        yy = jnp.dot(s_ref[sb].astype(BF16), rc_ref[jj], preferred_element_type=F32)
        y_ref[sb, 0] = yy[:, :LANES]
        y_ref[sb, 1] = yy[:, LANES:]
        for j2 in range(2):
            j = 2 * jj + j2
            for b in range(nb):
                yn_ref[b * tc:(b + 1) * tc, j * LANES:(j + 1) * LANES] = (
                    y_ref[sb, j2, pl.ds(b * 2 + j2, tc, stride=rows_per_t), :])

    def s5_out():
        y = yn_ref[...] + du_ref[...]
        y = jax.nn.gelu(y)
        z = jnp.dot(y.astype(BF16), wglu_ref[...], preferred_element_type=F32) + bglu_ref[...]
        y = y * jax.nn.sigmoid(z)
        o_ref[:, :, :w] = _rms(y, gs_ref[...]).astype(BF16).reshape(nb, tc, w)

    def pool(b):
        zs = []
        ssq = jnp.zeros((tc, 1), F32)
        for gi, win in enumerate(POOL_WINDOWS):
            e = ext_ref[b, :, gi * gw:(gi + 1) * gw]
            s = e
            shift = 1
            while shift < win:
                s = s + pltpu.roll(s, shift, axis=0)
                shift *= 2
            cnt = jnp.minimum(t_idx + 1, win).astype(F32)
            pooled = s[POOL_HALO:] / cnt - e[POOL_HALO:]
            z = jnp.dot(pooled.astype(BF16), wp_ref[gi], preferred_element_type=F32)
            z = z * sc_ref[:, gi * gw:(gi + 1) * gw]
            ssq = ssq + jnp.sum(z * z, axis=-1, keepdims=True)
            zs.append(z)
        inv = lax.rsqrt(ssq / w + EPS)
        for gi in range(len(POOL_WINDOWS)):
            o_ref[b, :, w + gi * gw:w + (gi + 1) * gw] = (
                zs[gi] * inv * gp_ref[:, gi * gw:(gi + 1) * gw]).astype(BF16)
        ext_ref[b, :POOL_HALO, :] = ext_ref[b, tc:, :]

    for jj in range(min(nbuf, npair)):
        b_proj(jj)
    for jj in range(npair):
        if jj < npair - 1:
            next_u(jj)
        scan(jj)
        c_proj(jj)
        if jj + nbuf < npair:
            b_proj(jj + nbuf)
        for b in range(jj * nb // npair, (jj + 1) * nb // npair):
            pool(b)
    next_u(npair - 1)
    s5_out()


def _mixer(h3, gmix, win, rb, rc, at, dskip, wglu, bglu, gs, wp, sc, gp):
    tc = TILES["mixer"]
    nb, L, d = h3.shape
    w = d // 2
    npair = rb.shape[0]
    nstate = rb.shape[2]
    rows = tc * 2 * nb
    nbuf = 2
    scratch = [
        pltpu.VMEM((nb * tc, d), F32),
        pltpu.VMEM((npair, 2, rows, LANES), F32),
        pltpu.VMEM((nbuf, rows, nstate), F32),
        pltpu.VMEM((nbuf, 2, rows, LANES), F32),
        pltpu.VMEM((nb * tc, w), F32),
        pltpu.VMEM((npair, 2 * nb, nstate), F32),
        pltpu.VMEM((nb, tc + POOL_HALO, w), F32),
        pltpu.VMEM((nb * tc, w), F32),
        pltpu.VMEM((nb * tc, d), BF16),
    ]
    blocks = (3 * nb * tc * d * 4 + 3 * nb * tc * d * 2 + d * d * 2 + rb.size * 2 + rc.size * 2
              + at.size * 4 + w * w * 2 + wp.size * 2
              + (nb * tc * d + npair * 2 * rows * LANES + nbuf * rows * nstate + nbuf * 2 * rows * LANES
                 + 2 * nb * tc * w + npair * 2 * nb * nstate + nb * (tc + POOL_HALO) * w) * 4)
    temps = nb * tc * d * 6 + rows * nstate * 6 + 4 * nb * tc * w * 4
    nchunks = L // tc
    return pl.pallas_call(
        functools.partial(_mixer_kernel, tc=tc, nb=nb, npair=npair),
        grid=(nchunks,),
        in_specs=[
            pl.BlockSpec((nb, tc, d), lambda c: (0, 0, 0), pipeline_mode=pl.Buffered(1)),
            pl.BlockSpec((nb, tc, d), lambda c: (0, jnp.minimum(c + 1, nchunks - 1), 0)),
            _const_spec((1, d)),
            _const_spec((d, d)),
            _const_spec(rb.shape),
            _const_spec(rc.shape),
            _const_spec(at.shape),
            _const_spec((1, w)),
            _const_spec((w, w)),
            _const_spec((1, w)),
            _const_spec((1, w)),
            _const_spec(wp.shape),
            _const_spec((1, w)),
            _const_spec((1, w)),
        ],
        out_specs=pl.BlockSpec((nb, tc, d), lambda c: (0, c, 0)),
        out_shape=jax.ShapeDtypeStruct((nb, L, d), BF16),
        scratch_shapes=scratch,
        compiler_params=_params(blocks, temps, 1),
        name="mixer",
    )(h3, h3, gmix, win, rb, rc, at, dskip, wglu, bglu, gs, wp, sc, gp)


def _merge_proj_kernel(h_ref, m_ref, w_ref, o_ref):
    o_ref[...] = h_ref[...] + jnp.dot(m_ref[...], w_ref[...], preferred_element_type=F32)


def _merge_proj(h, m, w):
    tm = TILES["proj"]
    n, d = h.shape
    blocks = 4 * tm * d * 4 + 2 * tm * d * 2 + d * d * 2
    temps = tm * d * 4
    return pl.pallas_call(
        _merge_proj_kernel,
        grid=(n // tm,),
        in_specs=[
            pl.BlockSpec((tm, d), lambda i: (i, 0)),
            pl.BlockSpec((tm, d), lambda i: (i, 0)),
            _const_spec((d, d)),
        ],
        out_specs=pl.BlockSpec((tm, d), lambda i: (i, 0)),
        out_shape=jax.ShapeDtypeStruct((n, d), F32),
        compiler_params=_params(blocks, temps, 1),
        name="merge_proj",
    )(h, m, w)


def _kv_kernel(m_ref, g_ref, wk_ref, wv_ref, k_ref, v_ref, mn_ref):
    @pl.when(pl.program_id(0) == 0)
    def _():
        mn_ref[...] = _rms(m_ref[...], g_ref[...]).astype(BF16)

    mn = mn_ref[...]
    k_ref[...] = jnp.dot(mn, wk_ref[...], preferred_element_type=F32).astype(BF16)
    v_ref[...] = jnp.dot(mn, wv_ref[...], preferred_element_type=F32).astype(BF16)


def _kv(mem2, g, wk, wv):
    tn = TILES["kv"]
    n, d = mem2.shape
    blocks = n * d * 4 + n * d * 2 + 4 * d * tn * 2 + 4 * n * tn * 2
    temps = 2 * n * tn * 4 + n * d * 4
    return pl.pallas_call(
        _kv_kernel,
        grid=(d // tn,),
        in_specs=[
            _const_spec((n, d)),
            _const_spec((1, d)),
            pl.BlockSpec((d, tn), lambda j: (0, j)),
            pl.BlockSpec((d, tn), lambda j: (0, j)),
        ],
        out_specs=(pl.BlockSpec((n, tn), lambda j: (0, j)), pl.BlockSpec((n, tn), lambda j: (0, j))),
        out_shape=(jax.ShapeDtypeStruct((n, d), BF16), jax.ShapeDtypeStruct((n, d), BF16)),
        scratch_shapes=[pltpu.VMEM((n, d), BF16)],
        compiler_params=_params(blocks, temps, 1),
        name="mem_kv",
    )(mem2, g, wk, wv)


def _xattn_kernel(h_ref, g_ref, wq_ref, k_ref, v_ref, wo_ref, o_ref):
    h = h_ref[...]
    d = h.shape[-1]
    hd = d // MEM_HEADS
    q = jnp.dot(_rms(h, g_ref[...]).astype(BF16), wq_ref[...], preferred_element_type=F32)
    outs = []
    for hh in range(MEM_HEADS):
        qh = q[:, hh * hd:(hh + 1) * hd].astype(BF16)
        kh = k_ref[:, hh * hd:(hh + 1) * hd]
        s = lax.dot_general(qh, kh, (((1,), (1,)), ((), ())), preferred_element_type=F32)
        s = s * (hd ** -0.5)
        e = jnp.exp(s - jnp.max(s, axis=-1, keepdims=True))
        p = e / jnp.sum(e, axis=-1, keepdims=True)
        outs.append(jnp.dot(p.astype(BF16), v_ref[:, hh * hd:(hh + 1) * hd],
                            preferred_element_type=F32).astype(BF16))
    o = jnp.concatenate(outs, axis=-1)
    o_ref[...] = h + jnp.dot(o, wo_ref[...], preferred_element_type=F32)


def _xattn(h, g, wq, k, v, wo, *, nb):
    tm = TILES["xattn"]
    n, d = h.shape
    nm = k.shape[0] // nb
    tiles = n // nb // tm
    blocks = 4 * tm * d * 4 + 2 * d * d * 2 + 4 * nm * d * 2
    temps = 4 * tm * d * 4
    return pl.pallas_call(
        _xattn_kernel,
        grid=(nb, tiles),
        in_specs=[
            pl.BlockSpec((tm, d), lambda b, i: (b * tiles + i, 0)),
            _const_spec((1, d)),
            _const_spec((d, d)),
            pl.BlockSpec((nm, d), lambda b, i: (b, 0)),
            pl.BlockSpec((nm, d), lambda b, i: (b, 0)),
            _const_spec((d, d)),
        ],
        out_specs=pl.BlockSpec((tm, d), lambda b, i: (b * tiles + i, 0)),
        out_shape=jax.ShapeDtypeStruct((n, d), F32),
        compiler_params=_params(blocks, temps, 2),
        name="xattn",
    )(h, g, wq, k, v, wo)


def _s5_operands(abar_re, abar_im, bbar_re_hgp, bbar_im_hgp, c_re, c_im, nb):
    g, p = abar_re.shape
    h = bbar_re_hgp.shape[0]
    gl = LANES // h
    npair = g // (2 * gl)
    eye = jnp.eye(gl, dtype=F32)
    bb = jnp.stack([bbar_re_hgp, bbar_im_hgp]).reshape(2, h, npair, 2, gl, p)
    rb = jnp.einsum('rhjkgp,gq->jkghrqp', bb, eye).reshape(npair, 2 * LANES, 2 * gl * p)
    cc = jnp.stack([c_re, -c_im]).reshape(2, npair, 2, gl, h, p)
    rc = jnp.einsum('rjkqhp,gq->jrgpkqh', cc, eye).reshape(npair, 2 * gl * p, 2 * LANES)
    ab = jnp.stack([abar_re, abar_im]).reshape(2, npair, 2, gl, p)
    at = jnp.transpose(ab, (1, 2, 0, 3, 4)).reshape(npair, 1, 2, 2 * gl * p)
    at = jnp.broadcast_to(at, (npair, nb, 2, 2 * gl * p)).reshape(npair, 2 * nb, 2 * gl * p)
    return rb.astype(BF16), rc.astype(BF16), at


def kernel(x, mem, g_ffn1, w1_gate, w1_up, w1_down, g_mix, w_in, ssm_a_re, ssm_a_im, ssm_log_dt,
           ssm_b_re, ssm_b_im, ssm_c_re, ssm_c_im, ssm_d, w_glu, b_glu, w_pool, pool_scale,
           g_out_ssm, g_out_pool, w_out, g_xattn, g_mem, w_q, w_k, w_v, w_o,
           g_ffn2, w2_gate, w2_up, w2_down, g_final):
    nb, L, d = x.shape
    n = nb * L
    depth = g_ffn1.shape[0]
    bf = lambda a: a.astype(BF16)
    row = lambda a: a.reshape(1, -1)

    h = x.reshape(n, d)
    for l in range(depth):
        last = l == depth - 1
        pool_shape = w_pool[l].shape
        later = [w2_gate[l], w2_up[l], w2_down[l], w_in[l], w_out[l], w_q[l], w_k[l], w_v[l], w_o[l],
                 w_glu[l], w_pool[l].reshape(-1, pool_shape[-1])]
        h, later = _ffn(h, row(g_ffn1[l]), bf(w1_gate[l]), bf(w1_up[l]), bf(w1_down[l]), row(g_final),
                        final_norm=False, side=later)
        w2g, w2u, w2d, win, wout, wq, wk, wv, wo, wglu, wpool = later

        abar_re, abar_im, bbar_re, bbar_im = _s5_prep(
            ssm_a_re[l], ssm_a_im[l], ssm_log_dt[l],
            jnp.transpose(ssm_b_re[l], (2, 0, 1)), jnp.transpose(ssm_b_im[l], (2, 0, 1)))
        rb, rc, at = _s5_operands(abar_re, abar_im, bbar_re, bbar_im, ssm_c_re[l], ssm_c_im[l], nb)
        merged = _mixer(h.reshape(nb, L, d), row(g_mix[l]), win, rb, rc, at, row(ssm_d[l]),
                        wglu, row(b_glu[l]), row(g_out_ssm[l]),
                        wpool.reshape(pool_shape), row(pool_scale[l]), row(g_out_pool[l]))
        h = _merge_proj(h, merged.reshape(n, d), wout)

        k, v = _kv(mem.reshape(nb * mem.shape[1], d), row(g_mem[l]), wk, wv)
        h = _xattn(h, row(g_xattn[l]), wq, k, v, wo, nb=nb)

        h, _ = _ffn(h, row(g_ffn2[l]), w2g, w2u, w2d, row(g_final), final_norm=last)
    return h.reshape(nb, L, d)
```

```python
import functools

import jax
import jax.numpy as jnp
from jax import lax
from jax.experimental import pallas as pl
from jax.experimental.pallas import tpu as pltpu

F32 = jnp.float32
BF16 = jnp.bfloat16

EPS = 1e-6
POOL_WINDOWS = (2, 4, 8, 16)
MEM_HEADS = 4

LANES = 128
SUBLANES = 8
MXU_COLS = 256
VMEM_BYTES_V7X = 64 * 1024 * 1024
VMEM_RESERVE = 2 * 1024 * 1024
POOL_HALO = 16

TILES = dict(ffn=(1024, 512), ffn_head=(1024, 256), mixer=128, proj=1024, xattn=512, kv=512)


def _params(block_bytes, temp_bytes, ndims):
    return pltpu.CompilerParams(
        dimension_semantics=("arbitrary",) * ndims,
        vmem_limit_bytes=int(min(VMEM_BYTES_V7X - VMEM_RESERVE, block_bytes + temp_bytes)),
    )


def _rms(x, g):
    return x * lax.rsqrt(jnp.mean(x * x, axis=-1, keepdims=True) + EPS) * g


def _const_spec(shape):
    nd = len(shape)
    return pl.BlockSpec(shape, lambda *_: (0,) * nd, pipeline_mode=pl.Buffered(1))


def _ffn_step(k, x_ref, g_ref, wg_ref, wu_ref, wd_ref, o_ref, xn_ref):
    @pl.when(k == 0)
    def _():
        x = x_ref[...]
        xn_ref[...] = _rms(x, g_ref[...]).astype(BF16)
        o_ref[...] = x

    xn = xn_ref[...]
    tf = wg_ref.shape[1]
    hf = min(tf, MXU_COLS)
    acts = []
    for p in range(tf // hf):
        gate = jnp.dot(xn, wg_ref[:, p * hf:(p + 1) * hf], preferred_element_type=F32)
        up = jnp.dot(xn, wu_ref[:, p * hf:(p + 1) * hf], preferred_element_type=F32)
        acts.append((jax.nn.silu(gate) * up * 0.5).astype(BF16))
    acc = None
    for p, act in enumerate(acts):
        part = jnp.dot(act, wd_ref[p * hf:(p + 1) * hf, :], preferred_element_type=F32)
        acc = part if acc is None else acc + part
    o_ref[...] += acc


def _ffn_kernel(*refs, final_norm, n_side, adopt):
    x_ref, g_ref, wg_ref, wu_ref, wd_ref, gf_ref = refs[:6]
    n_in = 6 + adopt
    side_in = refs[n_in:n_in + n_side]
    o_ref = refs[n_in + n_side]
    side_out = refs[n_in + n_side + 1:n_in + 2 * n_side + 1]
    xn_ref = refs[n_in + 2 * n_side + 1]
    i = pl.program_id(0)
    k = pl.program_id(1)

    if adopt:
        head_ref, sem = refs[6], refs[n_in + 2 * n_side + 2]

        @pl.when((i == 0) & (k == 0))
        def _():
            cp = pltpu.make_async_copy(head_ref, o_ref, sem)
            cp.start()
            cp.wait()

        @pl.when(i > 0)
        def _():
            _ffn_step(k, x_ref, g_ref, wg_ref, wu_ref, wd_ref, o_ref, xn_ref)
    else:
        _ffn_step(k, x_ref, g_ref, wg_ref, wu_ref, wd_ref, o_ref, xn_ref)

    for src, dst in zip(side_in, side_out):
        dst[...] = src[...].astype(BF16)

    if final_norm:
        @pl.when(k == pl.num_programs(1) - 1)
        def _():
            o_ref[...] = _rms(o_ref[...], gf_ref[...])


def _ffn_head_kernel(x_ref, g_ref, wg_ref, wu_ref, wd_ref, o_ref, wgb_ref, wub_ref, wdb_ref, xn_ref):
    wgb_ref[...] = wg_ref[...].astype(BF16)
    wub_ref[...] = wu_ref[...].astype(BF16)
    wdb_ref[...] = wd_ref[...].astype(BF16)
    _ffn_step(pl.program_id(0), x_ref, g_ref, wgb_ref, wub_ref, wdb_ref, o_ref, xn_ref)


def _ffn_head(x, g, wg, wu, wd):
    tm, tf = TILES["ffn_head"]
    d = x.shape[1]
    dff = wg.shape[1]
    blocks = 2 * (tm * d * 4) * 2 + tm * d * 2 + 2 * 3 * d * tf * (4 + 2)
    temps = 3 * tm * tf * 4 + tm * d * 4
    col = lambda k: (0, k)
    row = lambda k: (k, 0)
    return pl.pallas_call(
        _ffn_head_kernel,
        grid=(dff // tf,),
        in_specs=[
            pl.BlockSpec((tm, d), lambda k: (0, 0)),
            _const_spec((1, d)),
            pl.BlockSpec((d, tf), col),
            pl.BlockSpec((d, tf), col),
            pl.BlockSpec((tf, d), row),
        ],
        out_specs=[
            pl.BlockSpec((tm, d), lambda k: (0, 0)),
            pl.BlockSpec((d, tf), col),
            pl.BlockSpec((d, tf), col),
            pl.BlockSpec((tf, d), row),
        ],
        out_shape=[
            jax.ShapeDtypeStruct((tm, d), F32),
            jax.ShapeDtypeStruct(wg.shape, BF16),
            jax.ShapeDtypeStruct(wu.shape, BF16),
            jax.ShapeDtypeStruct(wd.shape, BF16),
        ],
        scratch_shapes=[pltpu.VMEM((tm, d), BF16)],
        compiler_params=_params(blocks, temps, 1),
        name="ffn_head",
    )(x, g, wg, wu, wd)


def _cast_spec(shape, ni, nk):
    r, c = shape
    row_align = 2 * SUBLANES
    if c % (ni * LANES) == 0 and r % (nk * row_align) == 0:
        return pl.BlockSpec((r // nk, c // ni), lambda i, k: (k, i))
    m = max(m for m in range(1, nk + 1) if c % (m * LANES) == 0)
    assert r % (ni * row_align) == 0, shape
    return pl.BlockSpec((r // ni, c // m), lambda i, k: (i, jnp.minimum(k, m - 1)))


def _ffn(x, g, wg, wu, wd, gf, *, final_norm, side=(), head=None):
    tm, tf = TILES["ffn"]
    n, d = x.shape
    dff = wg.shape[1]
    ni, nk = n // tm, dff // tf
    adopt = head is not None
    if adopt:
        assert head.shape == (tm, d) and TILES["ffn_head"][0] == tm
        xi = lambda i: jnp.maximum(i, 1)
        wk = lambda i, k: jnp.where(i == 0, 0, k)
    else:
        xi = lambda i: i
        wk = lambda i, k: k
    side_specs = [_cast_spec(w.shape, ni, nk) for w in side]
    side_bytes = sum(2 * sp.block_shape[0] * sp.block_shape[1] * (4 + 2) for sp in side_specs)
    blocks = 2 * (tm * d * 4) * 2 + tm * d * 2 + 2 * 3 * d * tf * 2 + side_bytes
    temps = 3 * tm * tf * 4 + tm * d * 4
    outs = pl.pallas_call(
        functools.partial(_ffn_kernel, final_norm=final_norm, n_side=len(side), adopt=adopt),
        grid=(ni, nk),
        in_specs=[
            pl.BlockSpec((tm, d), lambda i, k: (xi(i), 0)),
            _const_spec((1, d)),
            pl.BlockSpec((d, tf), lambda i, k: (0, wk(i, k))),
            pl.BlockSpec((d, tf), lambda i, k: (0, wk(i, k))),
            pl.BlockSpec((tf, d), lambda i, k: (wk(i, k), 0)),
            _const_spec((1, d)),
        ] + ([pl.BlockSpec(memory_space=pl.ANY)] if adopt else []) + side_specs,
        out_specs=[pl.BlockSpec((tm, d), lambda i, k: (i, 0))] + side_specs,
        out_shape=[jax.ShapeDtypeStruct((n, d), F32)]
        + [jax.ShapeDtypeStruct(w.shape, BF16) for w in side],
        scratch_shapes=[pltpu.VMEM((tm, d), BF16)] + ([pltpu.SemaphoreType.DMA(())] if adopt else []),
        compiler_params=_params(blocks, temps, 2),
        name="ffn",
    )(x, g, wg, wu, wd, gf, *([head] if adopt else []), *side)
    return outs[0], list(outs[1:])


def _s5_prep_kernel(lr_ref, li_ref, ldt_ref, br_ref, bi_ref, ar_ref, ai_ref, bbr_ref, bbi_ref):
    lr = lr_ref[...]
    li = li_ref[...]
    dt = jnp.exp(ldt_ref[...])
    mag = jnp.exp(lr * dt)
    abar_re = mag * jnp.cos(li * dt)
    abar_im = mag * jnp.sin(li * dt)
    nr, ni = abar_re - 1.0, abar_im
    den = lr * lr + li * li
    fr = (nr * lr + ni * li) / den
    fi = (ni * lr - nr * li) / den
    ar_ref[...] = abar_re
    ai_ref[...] = abar_im
    for h in range(br_ref.shape[0]):
        br = br_ref[h]
        bi = bi_ref[h]
        bbr_ref[h] = fr * br - fi * bi
        bbi_ref[h] = fr * bi + fi * br


def _s5_prep(a_re, a_im, log_dt, b_re_hgp, b_im_hgp):
    g, p = a_re.shape
    h = b_re_hgp.shape[0]
    return pl.pallas_call(
        _s5_prep_kernel,
        out_shape=(
            jax.ShapeDtypeStruct((g, p), F32),
            jax.ShapeDtypeStruct((g, p), F32),
            jax.ShapeDtypeStruct((h, g, p), F32),
            jax.ShapeDtypeStruct((h, g, p), F32),
        ),
        name="s5_prep",
    )(a_re, a_im, log_dt.reshape(g, 1), b_re_hgp, b_im_hgp)


def _mixer_kernel(h0_ref, hn_ref, gmix_ref, win_ref, rb_ref, rc_ref, at_ref, d_ref, wglu_ref, bglu_ref,
                  gs_ref, wp_ref, sc_ref, gp_ref, o_ref,
                  u_ref, ul_ref, s_ref, y_ref, yn_ref, st_ref, ext_ref, du_ref, xn_ref, *,
                  tc, nb, npair):
    c = pl.program_id(0)
    d = hn_ref.shape[-1]
    w = d // 2
    half = s_ref.shape[-1] // 2
    rows_per_t = 2 * nb
    nbuf = s_ref.shape[0]

    def in_proj(h_ref):
        h = h_ref[...].reshape(nb * tc, d)
        return jnp.dot(_rms(h, gmix_ref[...]).astype(BF16), win_ref[...], preferred_element_type=F32)

    @pl.when(c == 0)
    def _():
        ul_ref[...] = jnp.zeros(ul_ref.shape, F32)
        st_ref[...] = jnp.zeros(st_ref.shape, F32)
        ext_ref[:, :POOL_HALO, :] = jnp.zeros((nb, POOL_HALO, w), F32)
        u_ref[...] = in_proj(h0_ref)

    for jj in range(npair):
        for j2 in range(2):
            j = 2 * jj + j2
            for b in range(nb):
                ul_ref[jj, j2, pl.ds(b * 2 + j2, tc, stride=rows_per_t), :] = (
                    u_ref[b * tc:(b + 1) * tc, j * LANES:(j + 1) * LANES])
    du_ref[...] = d_ref[...] * u_ref[:, :w]
    for b in range(nb):
        ext_ref[b, POOL_HALO:, :] = u_ref[b * tc:(b + 1) * tc, w:]

    xn_ref[...] = _rms(hn_ref[...].reshape(nb * tc, d), gmix_ref[...]).astype(BF16)
    pw = d // npair
    gw = w // len(POOL_WINDOWS)
    t_idx = c * tc + lax.broadcasted_iota(jnp.int32, (tc, 1), 0)

    def next_u(p):
        u_ref[:, p * pw:(p + 1) * pw] = jnp.dot(
            xn_ref[...], win_ref[:, p * pw:(p + 1) * pw], preferred_element_type=F32)

    def b_proj(jj):
        lhs = jnp.concatenate([ul_ref[jj, 0], ul_ref[jj, 1]], axis=-1).astype(BF16)
        s_ref[jj % nbuf] = jnp.dot(lhs, rb_ref[jj], preferred_element_type=F32)

    def scan(jj):
        sb = jj % nbuf
        a_r = at_ref[jj, :, :half]
        a_i = at_ref[jj, :, half:]
        s_r = st_ref[jj, :, :half]
        s_i = st_ref[jj, :, half:]
        for t in range(tc):
            r0 = t * rows_per_t
            n_r = a_r * s_r - a_i * s_i + s_ref[sb, r0:r0 + rows_per_t, :half]
            n_i = a_r * s_i + a_i * s_r + s_ref[sb, r0:r0 + rows_per_t, half:]
            s_ref[sb, r0:r0 + rows_per_t, :half] = n_r
            s_ref[sb, r0:r0 + rows_per_t, half:] = n_i
            s_r, s_i = n_r, n_i
        st_ref[jj, :, :half] = s_r
        st_ref[jj, :, half:] = s_i

    def c_proj(jj):
        sb = jj % nbuf
        yy = jnp.dot(s_ref[sb].astype(BF16), rc_ref[jj], preferred_element_type=F32)
        y_ref[sb, 0] = yy[:, :LANES]
        y_ref[sb, 1] = yy[:, LANES:]
        for j2 in range(2):
            j = 2 * jj + j2
            for b in range(nb):
                yn_ref[b * tc:(b + 1) * tc, j * LANES:(j + 1) * LANES] = (
                    y_ref[sb, j2, pl.ds(b * 2 + j2, tc, stride=rows_per_t), :])

    def s5_out():
        y = yn_ref[...] + du_ref[...]
        y = jax.nn.gelu(y)
        z = jnp.dot(y.astype(BF16), wglu_ref[...], preferred_element_type=F32) + bglu_ref[...]
        y = y * jax.nn.sigmoid(z)
        o_ref[:, :, :w] = _rms(y, gs_ref[...]).astype(BF16).reshape(nb, tc, w)

    def pool(b):
        zs = []
        ssq = jnp.zeros((tc, 1), F32)
        for gi, win in enumerate(POOL_WINDOWS):
            e = ext_ref[b, :, gi * gw:(gi + 1) * gw]
            s = e
            shift = 1
            while shift < win:
                s = s + pltpu.roll(s, shift, axis=0)
                shift *= 2
            cnt = jnp.minimum(t_idx + 1, win).astype(F32)
            pooled = s[POOL_HALO:] / cnt - e[POOL_HALO:]
            z = jnp.dot(pooled.astype(BF16), wp_ref[gi], preferred_element_type=F32)
            z = z * sc_ref[:, gi * gw:(gi + 1) * gw]
            ssq = ssq + jnp.sum(z * z, axis=-1, keepdims=True)
            zs.append(z)
        inv = lax.rsqrt(ssq / w + EPS)
        for gi in range(len(POOL_WINDOWS)):
            o_ref[b, :, w + gi * gw:w + (gi + 1) * gw] = (
                zs[gi] * inv * gp_ref[:, gi * gw:(gi + 1) * gw]).astype(BF16)
        ext_ref[b, :POOL_HALO, :] = ext_ref[b, tc:, :]

    for jj in range(min(nbuf, npair)):
        b_proj(jj)
    for jj in range(npair):
        if jj < npair - 1:
            next_u(jj)
        scan(jj)
        c_proj(jj)
        if jj + nbuf < npair:
            b_proj(jj + nbuf)
        for b in range(jj * nb // npair, (jj + 1) * nb // npair):
            pool(b)
    next_u(npair - 1)
    s5_out()


def _mixer(h3, gmix, win, rb, rc, at, dskip, wglu, bglu, gs, wp, sc, gp):
    tc = TILES["mixer"]
    nb, L, d = h3.shape
    w = d // 2
    npair = rb.shape[0]
    nstate = rb.shape[2]
    rows = tc * 2 * nb
    nbuf = 2
    scratch = [
        pltpu.VMEM((nb * tc, d), F32),
        pltpu.VMEM((npair, 2, rows, LANES), F32),
        pltpu.VMEM((nbuf, rows, nstate), F32),
        pltpu.VMEM((nbuf, 2, rows, LANES), F32),
        pltpu.VMEM((nb * tc, w), F32),
        pltpu.VMEM((npair, 2 * nb, nstate), F32),
        pltpu.VMEM((nb, tc + POOL_HALO, w), F32),
        pltpu.VMEM((nb * tc, w), F32),
        pltpu.VMEM((nb * tc, d), BF16),
    ]
    blocks = (3 * nb * tc * d * 4 + 3 * nb * tc * d * 2 + d * d * 2 + rb.size * 2 + rc.size * 2
              + at.size * 4 + w * w * 2 + wp.size * 2
              + (nb * tc * d + npair * 2 * rows * LANES + nbuf * rows * nstate + nbuf * 2 * rows * LANES
                 + 2 * nb * tc * w + npair * 2 * nb * nstate + nb * (tc + POOL_HALO) * w) * 4)
    temps = nb * tc * d * 6 + rows * nstate * 6 + 4 * nb * tc * w * 4
    nchunks = L // tc
    return pl.pallas_call(
        functools.partial(_mixer_kernel, tc=tc, nb=nb, npair=npair),
        grid=(nchunks,),
        in_specs=[
            pl.BlockSpec((nb, tc, d), lambda c: (0, 0, 0), pipeline_mode=pl.Buffered(1)),
            pl.BlockSpec((nb, tc, d), lambda c: (0, jnp.minimum(c + 1, nchunks - 1), 0)),
            _const_spec((1, d)),
            _const_spec((d, d)),
            _const_spec(rb.shape),
            _const_spec(rc.shape),
            _const_spec(at.shape),
            _const_spec((1, w)),
            _const_spec((w, w)),
            _const_spec((1, w)),
            _const_spec((1, w)),
            _const_spec(wp.shape),
            _const_spec((1, w)),
            _const_spec((1, w)),
        ],
        out_specs=pl.BlockSpec((nb, tc, d), lambda c: (0, c, 0)),
        out_shape=jax.ShapeDtypeStruct((nb, L, d), BF16),
        scratch_shapes=scratch,
        compiler_params=_params(blocks, temps, 1),
        name="mixer",
    )(h3, h3, gmix, win, rb, rc, at, dskip, wglu, bglu, gs, wp, sc, gp)


def _merge_proj_kernel(h_ref, m_ref, w_ref, o_ref):
    o_ref[...] = h_ref[...] + jnp.dot(m_ref[...], w_ref[...], preferred_element_type=F32)


def _merge_proj(h, m, w):
    tm = TILES["proj"]
    n, d = h.shape
    blocks = 4 * tm * d * 4 + 2 * tm * d * 2 + d * d * 2
    temps = tm * d * 4
    return pl.pallas_call(
        _merge_proj_kernel,
        grid=(n // tm,),
        in_specs=[
            pl.BlockSpec((tm, d), lambda i: (i, 0)),
            pl.BlockSpec((tm, d), lambda i: (i, 0)),
            _const_spec((d, d)),
        ],
        out_specs=pl.BlockSpec((tm, d), lambda i: (i, 0)),
        out_shape=jax.ShapeDtypeStruct((n, d), F32),
        compiler_params=_params(blocks, temps, 1),
        name="merge_proj",
    )(h, m, w)


def _kv_kernel(m_ref, g_ref, wk_ref, wv_ref, k_ref, v_ref, mn_ref):
    @pl.when(pl.program_id(0) == 0)
    def _():
        mn_ref[...] = _rms(m_ref[...], g_ref[...]).astype(BF16)

    mn = mn_ref[...]
    k_ref[...] = jnp.dot(mn, wk_ref[...], preferred_element_type=F32).astype(BF16)
    v_ref[...] = jnp.dot(mn, wv_ref[...], preferred_element_type=F32).astype(BF16)


def _kv(mem2, g, wk, wv):
    tn = TILES["kv"]
    n, d = mem2.shape
    blocks = n * d * 4 + n * d * 2 + 4 * d * tn * 2 + 4 * n * tn * 2
    temps = 2 * n * tn * 4 + n * d * 4
    return pl.pallas_call(
        _kv_kernel,
        grid=(d // tn,),
        in_specs=[
            _const_spec((n, d)),
            _const_spec((1, d)),
            pl.BlockSpec((d, tn), lambda j: (0, j)),
            pl.BlockSpec((d, tn), lambda j: (0, j)),
        ],
        out_specs=(pl.BlockSpec((n, tn), lambda j: (0, j)), pl.BlockSpec((n, tn), lambda j: (0, j))),
        out_shape=(jax.ShapeDtypeStruct((n, d), BF16), jax.ShapeDtypeStruct((n, d), BF16)),
        scratch_shapes=[pltpu.VMEM((n, d), BF16)],
        compiler_params=_params(blocks, temps, 1),
        name="mem_kv",
    )(mem2, g, wk, wv)


def _xattn_kernel(h_ref, g_ref, wq_ref, k_ref, v_ref, wo_ref, o_ref):
    h = h_ref[...]
    d = h.shape[-1]
    hd = d // MEM_HEADS
    q = jnp.dot(_rms(h, g_ref[...]).astype(BF16), wq_ref[...], preferred_element_type=F32)
    outs = []
    for hh in range(MEM_HEADS):
        qh = q[:, hh * hd:(hh + 1) * hd].astype(BF16)
        kh = k_ref[:, hh * hd:(hh + 1) * hd]
        s = lax.dot_general(qh, kh, (((1,), (1,)), ((), ())), preferred_element_type=F32)
        s = s * (hd ** -0.5)
        e = jnp.exp(s - jnp.max(s, axis=-1, keepdims=True))
        p = e / jnp.sum(e, axis=-1, keepdims=True)
        outs.append(jnp.dot(p.astype(BF16), v_ref[:, hh * hd:(hh + 1) * hd],
                            preferred_element_type=F32).astype(BF16))
    o = jnp.concatenate(outs, axis=-1)
    o_ref[...] = h + jnp.dot(o, wo_ref[...], preferred_element_type=F32)


def _xattn(h, g, wq, k, v, wo, *, nb):
    tm = TILES["xattn"]
    n, d = h.shape
    nm = k.shape[0] // nb
    tiles = n // nb // tm
    blocks = 4 * tm * d * 4 + 2 * d * d * 2 + 4 * nm * d * 2
    temps = 4 * tm * d * 4
    return pl.pallas_call(
        _xattn_kernel,
        grid=(nb, tiles),
        in_specs=[
            pl.BlockSpec((tm, d), lambda b, i: (b * tiles + i, 0)),
            _const_spec((1, d)),
            _const_spec((d, d)),
            pl.BlockSpec((nm, d), lambda b, i: (b, 0)),
            pl.BlockSpec((nm, d), lambda b, i: (b, 0)),
            _const_spec((d, d)),
        ],
        out_specs=pl.BlockSpec((tm, d), lambda b, i: (b * tiles + i, 0)),
        out_shape=jax.ShapeDtypeStruct((n, d), F32),
        compiler_params=_params(blocks, temps, 2),
        name="xattn",
    )(h, g, wq, k, v, wo)


def _s5_operands(abar_re, abar_im, bbar_re_hgp, bbar_im_hgp, c_re, c_im, nb):
    g, p = abar_re.shape
    h = bbar_re_hgp.shape[0]
    gl = LANES // h
    npair = g // (2 * gl)
    eye = jnp.eye(gl, dtype=F32)
    bb = jnp.stack([bbar_re_hgp, bbar_im_hgp]).reshape(2, h, npair, 2, gl, p)
    rb = jnp.einsum('rhjkgp,gq->jkghrqp', bb, eye).reshape(npair, 2 * LANES, 2 * gl * p)
    cc = jnp.stack([c_re, -c_im]).reshape(2, npair, 2, gl, h, p)
    rc = jnp.einsum('rjkqhp,gq->jrgpkqh', cc, eye).reshape(npair, 2 * gl * p, 2 * LANES)
    ab = jnp.stack([abar_re, abar_im]).reshape(2, npair, 2, gl, p)
    at = jnp.transpose(ab, (1, 2, 0, 3, 4)).reshape(npair, 1, 2, 2 * gl * p)
    at = jnp.broadcast_to(at, (npair, nb, 2, 2 * gl * p)).reshape(npair, 2 * nb, 2 * gl * p)
    return rb.astype(BF16), rc.astype(BF16), at


def kernel(x, mem, g_ffn1, w1_gate, w1_up, w1_down, g_mix, w_in, ssm_a_re, ssm_a_im, ssm_log_dt,
           ssm_b_re, ssm_b_im, ssm_c_re, ssm_c_im, ssm_d, w_glu, b_glu, w_pool, pool_scale,
           g_out_ssm, g_out_pool, w_out, g_xattn, g_mem, w_q, w_k, w_v, w_o,
           g_ffn2, w2_gate, w2_up, w2_down, g_final):
    nb, L, d = x.shape
    n = nb * L
    depth = g_ffn1.shape[0]
    bf = lambda a: a.astype(BF16)
    row = lambda a: a.reshape(1, -1)

    h = x.reshape(n, d)
    for l in range(depth):
        last = l == depth - 1
        pool_shape = w_pool[l].shape
        later = [w2_gate[l], w2_up[l], w2_down[l], w_in[l], w_out[l], w_q[l], w_k[l], w_v[l], w_o[l],
                 w_glu[l], w_pool[l].reshape(-1, pool_shape[-1])]
        head, w1g, w1u, w1d = _ffn_head(h, row(g_ffn1[l]), w1_gate[l], w1_up[l], w1_down[l])
        h, later = _ffn(h, row(g_ffn1[l]), w1g, w1u, w1d, row(g_final),
                        final_norm=False, side=later, head=head)
        w2g, w2u, w2d, win, wout, wq, wk, wv, wo, wglu, wpool = later

        abar_re, abar_im, bbar_re, bbar_im = _s5_prep(
            ssm_a_re[l], ssm_a_im[l], ssm_log_dt[l],
            jnp.transpose(ssm_b_re[l], (2, 0, 1)), jnp.transpose(ssm_b_im[l], (2, 0, 1)))
        rb, rc, at = _s5_operands(abar_re, abar_im, bbar_re, bbar_im, ssm_c_re[l], ssm_c_im[l], nb)
        merged = _mixer(h.reshape(nb, L, d), row(g_mix[l]), win, rb, rc, at, row(ssm_d[l]),
                        wglu, row(b_glu[l]), row(g_out_ssm[l]),
                        wpool.reshape(pool_shape), row(pool_scale[l]), row(g_out_pool[l]))
        h = _merge_proj(h, merged.reshape(n, d), wout)

        k, v = _kv(mem.reshape(nb * mem.shape[1], d), row(g_mem[l]), wk, wv)
        h = _xattn(h, row(g_xattn[l]), wq, k, v, wo, nb=nb)

        h, _ = _ffn(h, row(g_ffn2[l]), w2g, w2u, w2d, row(g_final), final_norm=last)
    return h.reshape(nb, L, d)
```

```python
import functools

import jax
import jax.numpy as jnp
from jax import lax
from jax.experimental import pallas as pl
from jax.experimental.pallas import tpu as pltpu

F32 = jnp.float32
BF16 = jnp.bfloat16

EPS = 1e-6
POOL_WINDOWS = (2, 4, 8, 16)
MEM_HEADS = 4

LANES = 128
SUBLANES = 8
MXU_COLS = 256
VMEM_BYTES_V7X = 64 * 1024 * 1024
VMEM_RESERVE = 2 * 1024 * 1024
POOL_HALO = 16

TILES = dict(ffn=(1024, 512), ffn_head=(1024, 256), mixer=128, proj=1024, xattn=512, kv=512)


def _params(block_bytes, temp_bytes, ndims):
    return pltpu.CompilerParams(
        dimension_semantics=("arbitrary",) * ndims,
        vmem_limit_bytes=int(min(VMEM_BYTES_V7X - VMEM_RESERVE, block_bytes + temp_bytes)),
    )


def _rms(x, g):
    return x * lax.rsqrt(jnp.mean(x * x, axis=-1, keepdims=True) + EPS) * g


def _const_spec(shape):
    nd = len(shape)
    return pl.BlockSpec(shape, lambda *_: (0,) * nd, pipeline_mode=pl.Buffered(1))


def _ffn_step(k, x_ref, g_ref, wg_ref, wu_ref, wd_ref, o_ref, xn_ref):
    def body(first):
        if first:
            xn_ref[...] = _rms(x_ref[...], g_ref[...]).astype(BF16)
        xn = xn_ref[...]
        tf = wg_ref.shape[1]
        hf = min(tf, MXU_COLS)
        acts = []
        for p in range(tf // hf):
            gate = jnp.dot(xn, wg_ref[:, p * hf:(p + 1) * hf], preferred_element_type=F32)
            up = jnp.dot(xn, wu_ref[:, p * hf:(p + 1) * hf], preferred_element_type=F32)
            acts.append((jax.nn.silu(gate) * up * 0.5).astype(BF16))
        acc = None
        for p, act in enumerate(acts):
            part = jnp.dot(act, wd_ref[p * hf:(p + 1) * hf, :], preferred_element_type=F32)
            acc = part if acc is None else acc + part
        o_ref[...] = (x_ref[...] if first else o_ref[...]) + acc

    pl.when(k == 0)(functools.partial(body, True))
    pl.when(k > 0)(functools.partial(body, False))


def _ffn_kernel(*refs, final_norm, n_side, adopt):
    x_ref, g_ref, wg_ref, wu_ref, wd_ref, gf_ref = refs[:6]
    n_in = 6 + adopt
    side_in = refs[n_in:n_in + n_side]
    o_ref = refs[n_in + n_side]
    side_out = refs[n_in + n_side + 1:n_in + 2 * n_side + 1]
    xn_ref = refs[n_in + 2 * n_side + 1]
    i = pl.program_id(0)
    k = pl.program_id(1)

    if adopt:
        head_ref, sem = refs[6], refs[n_in + 2 * n_side + 2]

        @pl.when((i == 0) & (k == 0))
        def _():
            cp = pltpu.make_async_copy(head_ref, o_ref, sem)
            cp.start()
            cp.wait()

        @pl.when(i > 0)
        def _():
            _ffn_step(k, x_ref, g_ref, wg_ref, wu_ref, wd_ref, o_ref, xn_ref)
    else:
        _ffn_step(k, x_ref, g_ref, wg_ref, wu_ref, wd_ref, o_ref, xn_ref)

    for src, dst in zip(side_in, side_out):
        dst[...] = src[...].astype(BF16)

    if final_norm:
        @pl.when(k == pl.num_programs(1) - 1)
        def _():
            o_ref[...] = _rms(o_ref[...], gf_ref[...])


def _ffn_head_kernel(x_ref, g_ref, wg_ref, wu_ref, wd_ref, o_ref, wgb_ref, wub_ref, wdb_ref, xn_ref):
    wgb_ref[...] = wg_ref[...].astype(BF16)
    wub_ref[...] = wu_ref[...].astype(BF16)
    wdb_ref[...] = wd_ref[...].astype(BF16)
    _ffn_step(pl.program_id(0), x_ref, g_ref, wgb_ref, wub_ref, wdb_ref, o_ref, xn_ref)


def _ffn_head(x, g, wg, wu, wd):
    tm, tf = TILES["ffn_head"]
    d = x.shape[1]
    dff = wg.shape[1]
    blocks = 2 * (tm * d * 4) * 2 + tm * d * 2 + 2 * 3 * d * tf * (4 + 2)
    temps = 3 * tm * tf * 4 + tm * d * 4
    col = lambda k: (0, k)
    row = lambda k: (k, 0)
    return pl.pallas_call(
        _ffn_head_kernel,
        grid=(dff // tf,),
        in_specs=[
            pl.BlockSpec((tm, d), lambda k: (0, 0)),
            _const_spec((1, d)),
            pl.BlockSpec((d, tf), col),
            pl.BlockSpec((d, tf), col),
            pl.BlockSpec((tf, d), row),
        ],
        out_specs=[
            pl.BlockSpec((tm, d), lambda k: (0, 0)),
            pl.BlockSpec((d, tf), col),
            pl.BlockSpec((d, tf), col),
            pl.BlockSpec((tf, d), row),
        ],
        out_shape=[
            jax.ShapeDtypeStruct((tm, d), F32),
            jax.ShapeDtypeStruct(wg.shape, BF16),
            jax.ShapeDtypeStruct(wu.shape, BF16),
            jax.ShapeDtypeStruct(wd.shape, BF16),
        ],
        scratch_shapes=[pltpu.VMEM((tm, d), BF16)],
        compiler_params=_params(blocks, temps, 1),
        name="ffn_head",
    )(x, g, wg, wu, wd)


def _cast_spec(shape, ni, nk):
    r, c = shape
    row_align = 2 * SUBLANES
    if c % (ni * LANES) == 0 and r % (nk * row_align) == 0:
        return pl.BlockSpec((r // nk, c // ni), lambda i, k: (k, i))
    m = max(m for m in range(1, nk + 1) if c % (m * LANES) == 0)
    assert r % (ni * row_align) == 0, shape
    return pl.BlockSpec((r // ni, c // m), lambda i, k: (i, jnp.minimum(k, m - 1)))


def _ffn(x, g, wg, wu, wd, gf, *, final_norm, side=(), head=None):
    tm, tf = TILES["ffn"]
    n, d = x.shape
    dff = wg.shape[1]
    ni, nk = n // tm, dff // tf
    adopt = head is not None
    if adopt:
        assert head.shape == (tm, d) and TILES["ffn_head"][0] == tm
        xi = lambda i: jnp.maximum(i, 1)
        wk = lambda i, k: jnp.where(i == 0, 0, k)
    else:
        xi = lambda i: i
        wk = lambda i, k: k
    side_specs = [_cast_spec(w.shape, ni, nk) for w in side]
    side_bytes = sum(2 * sp.block_shape[0] * sp.block_shape[1] * (4 + 2) for sp in side_specs)
    blocks = 2 * (tm * d * 4) * 2 + tm * d * 2 + 2 * 3 * d * tf * 2 + side_bytes
    temps = 3 * tm * tf * 4 + tm * d * 4
    outs = pl.pallas_call(
        functools.partial(_ffn_kernel, final_norm=final_norm, n_side=len(side), adopt=adopt),
        grid=(ni, nk),
        in_specs=[
            pl.BlockSpec((tm, d), lambda i, k: (xi(i), 0)),
            _const_spec((1, d)),
            pl.BlockSpec((d, tf), lambda i, k: (0, wk(i, k))),
            pl.BlockSpec((d, tf), lambda i, k: (0, wk(i, k))),
            pl.BlockSpec((tf, d), lambda i, k: (wk(i, k), 0)),
            _const_spec((1, d)),
        ] + ([pl.BlockSpec(memory_space=pl.ANY)] if adopt else []) + side_specs,
        out_specs=[pl.BlockSpec((tm, d), lambda i, k: (i, 0))] + side_specs,
        out_shape=[jax.ShapeDtypeStruct((n, d), F32)]
        + [jax.ShapeDtypeStruct(w.shape, BF16) for w in side],
        scratch_shapes=[pltpu.VMEM((tm, d), BF16)] + ([pltpu.SemaphoreType.DMA(())] if adopt else []),
        compiler_params=_params(blocks, temps, 2),
        name="ffn",
    )(x, g, wg, wu, wd, gf, *([head] if adopt else []), *side)
    return outs[0], list(outs[1:])


def _s5_prep_kernel(lr_ref, li_ref, ldt_ref, br_ref, bi_ref, ar_ref, ai_ref, bbr_ref, bbi_ref):
    lr = lr_ref[...]
    li = li_ref[...]
    dt = jnp.exp(ldt_ref[...])
    mag = jnp.exp(lr * dt)
    abar_re = mag * jnp.cos(li * dt)
    abar_im = mag * jnp.sin(li * dt)
    nr, ni = abar_re - 1.0, abar_im
    den = lr * lr + li * li
    fr = (nr * lr + ni * li) / den
    fi = (ni * lr - nr * li) / den
    ar_ref[...] = abar_re
    ai_ref[...] = abar_im
    for h in range(br_ref.shape[0]):
        br = br_ref[h]
        bi = bi_ref[h]
        bbr_ref[h] = fr * br - fi * bi
        bbi_ref[h] = fr * bi + fi * br


def _s5_prep(a_re, a_im, log_dt, b_re_hgp, b_im_hgp):
    g, p = a_re.shape
    h = b_re_hgp.shape[0]
    return pl.pallas_call(
        _s5_prep_kernel,
        out_shape=(
            jax.ShapeDtypeStruct((g, p), F32),
            jax.ShapeDtypeStruct((g, p), F32),
            jax.ShapeDtypeStruct((h, g, p), F32),
            jax.ShapeDtypeStruct((h, g, p), F32),
        ),
        name="s5_prep",
    )(a_re, a_im, log_dt.reshape(g, 1), b_re_hgp, b_im_hgp)


def _mixer_kernel(h0_ref, hn_ref, gmix_ref, win_ref, rb_ref, rc_ref, at_ref, d_ref, wglu_ref, bglu_ref,
                  gs_ref, wp_ref, sc_ref, gp_ref, o_ref,
                  u_ref, ul_ref, s_ref, y_ref, yn_ref, st_ref, ext_ref, du_ref, xn_ref, *,
                  tc, nb, npair):
    c = pl.program_id(0)
    d = hn_ref.shape[-1]
    w = d // 2
    half = s_ref.shape[-1] // 2
    rows_per_t = 2 * nb
    nbuf = s_ref.shape[0]

    def in_proj(h_ref):
        h = h_ref[...].reshape(nb * tc, d)
        return jnp.dot(_rms(h, gmix_ref[...]).astype(BF16), win_ref[...], preferred_element_type=F32)

    @pl.when(c == 0)
    def _():
        ul_ref[...] = jnp.zeros(ul_ref.shape, F32)
        st_ref[...] = jnp.zeros(st_ref.shape, F32)
        ext_ref[:, :POOL_HALO, :] = jnp.zeros((nb, POOL_HALO, w), F32)
        u_ref[...] = in_proj(h0_ref)

    for jj in range(npair):
        for j2 in range(2):
            j = 2 * jj + j2
            for b in range(nb):
                ul_ref[jj, j2, pl.ds(b * 2 + j2, tc, stride=rows_per_t), :] = (
                    u_ref[b * tc:(b + 1) * tc, j * LANES:(j + 1) * LANES])
    du_ref[...] = d_ref[...] * u_ref[:, :w]
    for b in range(nb):
        ext_ref[b, POOL_HALO:, :] = u_ref[b * tc:(b + 1) * tc, w:]

    xn_ref[...] = _rms(hn_ref[...].reshape(nb * tc, d), gmix_ref[...]).astype(BF16)
    pw = d // npair
    gw = w // len(POOL_WINDOWS)
    t_idx = c * tc + lax.broadcasted_iota(jnp.int32, (tc, 1), 0)

    def next_u(p):
        u_ref[:, p * pw:(p + 1) * pw] = jnp.dot(
            xn_ref[...], win_ref[:, p * pw:(p + 1) * pw], preferred_element_type=F32)

    def b_proj(jj):
        lhs = jnp.concatenate([ul_ref[jj, 0], ul_ref[jj, 1]], axis=-1).astype(BF16)
        s_ref[jj % nbuf] = jnp.dot(lhs, rb_ref[jj], preferred_element_type=F32)

    def scan(jj):
        sb = jj % nbuf
        a_r = at_ref[jj, :, :half]
        a_i = at_ref[jj, :, half:]
        s_r = st_ref[jj, :, :half]
        s_i = st_ref[jj, :, half:]
        for t in range(tc):
            r0 = t * rows_per_t
            n_r = a_r * s_r - a_i * s_i + s_ref[sb, r0:r0 + rows_per_t, :half]
            n_i = a_r * s_i + a_i * s_r + s_ref[sb, r0:r0 + rows_per_t, half:]
            s_ref[sb, r0:r0 + rows_per_t, :half] = n_r
            s_ref[sb, r0:r0 + rows_per_t, half:] = n_i
            s_r, s_i = n_r, n_i
        st_ref[jj, :, :half] = s_r
        st_ref[jj, :, half:] = s_i

    def c_proj(jj):
        sb = jj % nbuf
        yy = jnp.dot(s_ref[sb].astype(BF16), rc_ref[jj], preferred_element_type=F32)
        y_ref[sb, 0] = yy[:, :LANES]
        y_ref[sb, 1] = yy[:, LANES:]
        for j2 in range(2):
            j = 2 * jj + j2
            for b in range(nb):
                yn_ref[b * tc:(b + 1) * tc, j * LANES:(j + 1) * LANES] = (
                    y_ref[sb, j2, pl.ds(b * 2 + j2, tc, stride=rows_per_t), :])

    def s5_out():
        y = yn_ref[...] + du_ref[...]
        y = jax.nn.gelu(y)
        z = jnp.dot(y.astype(BF16), wglu_ref[...], preferred_element_type=F32) + bglu_ref[...]
        y = y * jax.nn.sigmoid(z)
        o_ref[:, :, :w] = _rms(y, gs_ref[...]).astype(BF16).reshape(nb, tc, w)

    def pool(b):
        zs = []
        ssq = jnp.zeros((tc, 1), F32)
        for gi, win in enumerate(POOL_WINDOWS):
            e = ext_ref[b, :, gi * gw:(gi + 1) * gw]
            s = e
            shift = 1
            while shift < win:
                s = s + pltpu.roll(s, shift, axis=0)
                shift *= 2
            cnt = jnp.minimum(t_idx + 1, win).astype(F32)
            pooled = s[POOL_HALO:] / cnt - e[POOL_HALO:]
            z = jnp.dot(pooled.astype(BF16), wp_ref[gi], preferred_element_type=F32)
            z = z * sc_ref[:, gi * gw:(gi + 1) * gw]
            ssq = ssq + jnp.sum(z * z, axis=-1, keepdims=True)
            zs.append(z)
        inv = lax.rsqrt(ssq / w + EPS)
        for gi in range(len(POOL_WINDOWS)):
            o_ref[b, :, w + gi * gw:w + (gi + 1) * gw] = (
                zs[gi] * inv * gp_ref[:, gi * gw:(gi + 1) * gw]).astype(BF16)
        ext_ref[b, :POOL_HALO, :] = ext_ref[b, tc:, :]

    for jj in range(min(nbuf, npair)):
        b_proj(jj)
    for jj in range(npair):
        if jj < npair - 1:
            next_u(jj)
        scan(jj)
        c_proj(jj)
        if jj + nbuf < npair:
            b_proj(jj + nbuf)
        for b in range(jj * nb // npair, (jj + 1) * nb // npair):
            pool(b)
    next_u(npair - 1)
    s5_out()


def _mixer(h3, gmix, win, rb, rc, at, dskip, wglu, bglu, gs, wp, sc, gp):
    tc = TILES["mixer"]
    nb, L, d = h3.shape
    w = d // 2
    npair = rb.shape[0]
    nstate = rb.shape[2]
    rows = tc * 2 * nb
    nbuf = 2
    scratch = [
        pltpu.VMEM((nb * tc, d), F32),
        pltpu.VMEM((npair, 2, rows, LANES), F32),
        pltpu.VMEM((nbuf, rows, nstate), F32),
        pltpu.VMEM((nbuf, 2, rows, LANES), F32),
        pltpu.VMEM((nb * tc, w), F32),
        pltpu.VMEM((npair, 2 * nb, nstate), F32),
        pltpu.VMEM((nb, tc + POOL_HALO, w), F32),
        pltpu.VMEM((nb * tc, w), F32),
        pltpu.VMEM((nb * tc, d), BF16),
    ]
    blocks = (3 * nb * tc * d * 4 + 3 * nb * tc * d * 2 + d * d * 2 + rb.size * 2 + rc.size * 2
              + at.size * 4 + w * w * 2 + wp.size * 2
              + (nb * tc * d + npair * 2 * rows * LANES + nbuf * rows * nstate + nbuf * 2 * rows * LANES
                 + 2 * nb * tc * w + npair * 2 * nb * nstate + nb * (tc + POOL_HALO) * w) * 4)
    temps = nb * tc * d * 6 + rows * nstate * 6 + 4 * nb * tc * w * 4
    nchunks = L // tc
    return pl.pallas_call(
        functools.partial(_mixer_kernel, tc=tc, nb=nb, npair=npair),
        grid=(nchunks,),
        in_specs=[
            pl.BlockSpec((nb, tc, d), lambda c: (0, 0, 0), pipeline_mode=pl.Buffered(1)),
            pl.BlockSpec((nb, tc, d), lambda c: (0, jnp.minimum(c + 1, nchunks - 1), 0)),
            _const_spec((1, d)),
            _const_spec((d, d)),
            _const_spec(rb.shape),
            _const_spec(rc.shape),
            _const_spec(at.shape),
            _const_spec((1, w)),
            _const_spec((w, w)),
            _const_spec((1, w)),
            _const_spec((1, w)),
            _const_spec(wp.shape),
            _const_spec((1, w)),
            _const_spec((1, w)),
        ],
        out_specs=pl.BlockSpec((nb, tc, d), lambda c: (0, c, 0)),
        out_shape=jax.ShapeDtypeStruct((nb, L, d), BF16),
        scratch_shapes=scratch,
        compiler_params=_params(blocks, temps, 1),
        name="mixer",
    )(h3, h3, gmix, win, rb, rc, at, dskip, wglu, bglu, gs, wp, sc, gp)


def _merge_proj_kernel(h_ref, m_ref, w_ref, o_ref):
    o_ref[...] = h_ref[...] + jnp.dot(m_ref[...], w_ref[...], preferred_element_type=F32)


def _merge_proj(h, m, w):
    tm = TILES["proj"]
    n, d = h.shape
    blocks = 4 * tm * d * 4 + 2 * tm * d * 2 + d * d * 2
    temps = tm * d * 4
    return pl.pallas_call(
        _merge_proj_kernel,
        grid=(n // tm,),
        in_specs=[
            pl.BlockSpec((tm, d), lambda i: (i, 0)),
            pl.BlockSpec((tm, d), lambda i: (i, 0)),
            _const_spec((d, d)),
        ],
        out_specs=pl.BlockSpec((tm, d), lambda i: (i, 0)),
        out_shape=jax.ShapeDtypeStruct((n, d), F32),
        compiler_params=_params(blocks, temps, 1),
        name="merge_proj",
    )(h, m, w)


def _kv_kernel(m_ref, g_ref, wk_ref, wv_ref, k_ref, v_ref, mn_ref):
    @pl.when(pl.program_id(0) == 0)
    def _():
        mn_ref[...] = _rms(m_ref[...], g_ref[...]).astype(BF16)

    mn = mn_ref[...]
    k_ref[...] = jnp.dot(mn, wk_ref[...], preferred_element_type=F32).astype(BF16)
    v_ref[...] = jnp.dot(mn, wv_ref[...], preferred_element_type=F32).astype(BF16)


def _kv(mem2, g, wk, wv):
    tn = TILES["kv"]
    n, d = mem2.shape
    blocks = n * d * 4 + n * d * 2 + 4 * d * tn * 2 + 4 * n * tn * 2
    temps = 2 * n * tn * 4 + n * d * 4
    return pl.pallas_call(
        _kv_kernel,
        grid=(d // tn,),
        in_specs=[
            _const_spec((n, d)),
            _const_spec((1, d)),
            pl.BlockSpec((d, tn), lambda j: (0, j)),
            pl.BlockSpec((d, tn), lambda j: (0, j)),
        ],
        out_specs=(pl.BlockSpec((n, tn), lambda j: (0, j)), pl.BlockSpec((n, tn), lambda j: (0, j))),
        out_shape=(jax.ShapeDtypeStruct((n, d), BF16), jax.ShapeDtypeStruct((n, d), BF16)),
        scratch_shapes=[pltpu.VMEM((n, d), BF16)],
        compiler_params=_params(blocks, temps, 1),
        name="mem_kv",
    )(mem2, g, wk, wv)


def _xattn_kernel(h_ref, g_ref, wq_ref, k_ref, v_ref, wo_ref, o_ref):
    h = h_ref[...]
    d = h.shape[-1]
    hd = d // MEM_HEADS
    q = jnp.dot(_rms(h, g_ref[...]).astype(BF16), wq_ref[...], preferred_element_type=F32)
    outs = []
    for hh in range(MEM_HEADS):
        qh = q[:, hh * hd:(hh + 1) * hd].astype(BF16)
        kh = k_ref[:, hh * hd:(hh + 1) * hd]
        s = lax.dot_general(qh, kh, (((1,), (1,)), ((), ())), preferred_element_type=F32)
        s = s * (hd ** -0.5)
        e = jnp.exp(s - jnp.max(s, axis=-1, keepdims=True))
        p = e / jnp.sum(e, axis=-1, keepdims=True)
        outs.append(jnp.dot(p.astype(BF16), v_ref[:, hh * hd:(hh + 1) * hd],
                            preferred_element_type=F32).astype(BF16))
    o = jnp.concatenate(outs, axis=-1)
    o_ref[...] = h + jnp.dot(o, wo_ref[...], preferred_element_type=F32)


def _xattn(h, g, wq, k, v, wo, *, nb):
    tm = TILES["xattn"]
    n, d = h.shape
    nm = k.shape[0] // nb
    tiles = n // nb // tm
    blocks = 4 * tm * d * 4 + 2 * d * d * 2 + 4 * nm * d * 2
    temps = 4 * tm * d * 4
    return pl.pallas_call(
        _xattn_kernel,
        grid=(nb, tiles),
        in_specs=[
            pl.BlockSpec((tm, d), lambda b, i: (b * tiles + i, 0)),
            _const_spec((1, d)),
            _const_spec((d, d)),
            pl.BlockSpec((nm, d), lambda b, i: (b, 0)),
            pl.BlockSpec((nm, d), lambda b, i: (b, 0)),
            _const_spec((d, d)),
        ],
        out_specs=pl.BlockSpec((tm, d), lambda b, i: (b * tiles + i, 0)),
        out_shape=jax.ShapeDtypeStruct((n, d), F32),
        compiler_params=_params(blocks, temps, 2),
        name="xattn",
    )(h, g, wq, k, v, wo)


def _s5_operands(abar_re, abar_im, bbar_re_hgp, bbar_im_hgp, c_re, c_im, nb):
    g, p = abar_re.shape
    h = bbar_re_hgp.shape[0]
    gl = LANES // h
    npair = g // (2 * gl)
    eye = jnp.eye(gl, dtype=F32)
    bb = jnp.stack([bbar_re_hgp, bbar_im_hgp]).reshape(2, h, npair, 2, gl, p)
    rb = jnp.einsum('rhjkgp,gq->jkghrqp', bb, eye).reshape(npair, 2 * LANES, 2 * gl * p)
    cc = jnp.stack([c_re, -c_im]).reshape(2, npair, 2, gl, h, p)
    rc = jnp.einsum('rjkqhp,gq->jrgpkqh', cc, eye).reshape(npair, 2 * gl * p, 2 * LANES)
    ab = jnp.stack([abar_re, abar_im]).reshape(2, npair, 2, gl, p)
    at = jnp.transpose(ab, (1, 2, 0, 3, 4)).reshape(npair, 1, 2, 2 * gl * p)
    at = jnp.broadcast_to(at, (npair, nb, 2, 2 * gl * p)).reshape(npair, 2 * nb, 2 * gl * p)
    return rb.astype(BF16), rc.astype(BF16), at


def kernel(x, mem, g_ffn1, w1_gate, w1_up, w1_down, g_mix, w_in, ssm_a_re, ssm_a_im, ssm_log_dt,
           ssm_b_re, ssm_b_im, ssm_c_re, ssm_c_im, ssm_d, w_glu, b_glu, w_pool, pool_scale,
           g_out_ssm, g_out_pool, w_out, g_xattn, g_mem, w_q, w_k, w_v, w_o,
           g_ffn2, w2_gate, w2_up, w2_down, g_final):
    nb, L, d = x.shape
    n = nb * L
    depth = g_ffn1.shape[0]
    bf = lambda a: a.astype(BF16)
    row = lambda a: a.reshape(1, -1)

    h = x.reshape(n, d)
    for l in range(depth):
        last = l == depth - 1
        pool_shape = w_pool[l].shape
        later = [w2_gate[l], w2_up[l], w2_down[l], w_in[l], w_out[l], w_q[l], w_k[l], w_v[l], w_o[l],
                 w_glu[l], w_pool[l].reshape(-1, pool_shape[-1])]
        head, w1g, w1u, w1d = _ffn_head(h, row(g_ffn1[l]), w1_gate[l], w1_up[l], w1_down[l])
        h, later = _ffn(h, row(g_ffn1[l]), w1g, w1u, w1d, row(g_final),
                        final_norm=False, side=later, head=head)
        w2g, w2u, w2d, win, wout, wq, wk, wv, wo, wglu, wpool = later

        abar_re, abar_im, bbar_re, bbar_im = _s5_prep(
            ssm_a_re[l], ssm_a_im[l], ssm_log_dt[l],
            jnp.transpose(ssm_b_re[l], (2, 0, 1)), jnp.transpose(ssm_b_im[l], (2, 0, 1)))
        rb, rc, at = _s5_operands(abar_re, abar_im, bbar_re, bbar_im, ssm_c_re[l], ssm_c_im[l], nb)
        merged = _mixer(h.reshape(nb, L, d), row(g_mix[l]), win, rb, rc, at, row(ssm_d[l]),
                        wglu, row(b_glu[l]), row(g_out_ssm[l]),
                        wpool.reshape(pool_shape), row(pool_scale[l]), row(g_out_pool[l]))
        h = _merge_proj(h, merged.reshape(n, d), wout)

        k, v = _kv(mem.reshape(nb * mem.shape[1], d), row(g_mem[l]), wk, wv)
        h = _xattn(h, row(g_xattn[l]), wq, k, v, wo, nb=nb)

        h, _ = _ffn(h, row(g_ffn2[l]), w2g, w2u, w2d, row(g_final), final_norm=last)
    return h.reshape(nb, L, d)
```

```python
import functools

import jax
import jax.numpy as jnp
from jax import lax
from jax.experimental import pallas as pl
from jax.experimental.pallas import tpu as pltpu

F32 = jnp.float32
BF16 = jnp.bfloat16

EPS = 1e-6
POOL_WINDOWS = (2, 4, 8, 16)
MEM_HEADS = 4

LANES = 128
SUBLANES = 8
MXU_COLS = 256
VMEM_BYTES_V7X = 64 * 1024 * 1024
VMEM_RESERVE = 2 * 1024 * 1024
POOL_HALO = 16

TILES = dict(ffn=(1024, 512), ffn_head=(1024, 256), mixer=128, proj=1024, xattn=512, kv=512)


def _params(block_bytes, temp_bytes, ndims):
    return pltpu.CompilerParams(
        dimension_semantics=("arbitrary",) * ndims,
        vmem_limit_bytes=int(min(VMEM_BYTES_V7X - VMEM_RESERVE, block_bytes + temp_bytes)),
    )


def _rms(x, g):
    return x * lax.rsqrt(jnp.mean(x * x, axis=-1, keepdims=True) + EPS) * g


def _const_spec(shape):
    nd = len(shape)
    return pl.BlockSpec(shape, lambda *_: (0,) * nd, pipeline_mode=pl.Buffered(1))


def _ffn_step(k, x_ref, g_ref, wg_ref, wu_ref, wd_ref, o_ref, xn_ref):
    def body(first):
        if first:
            xn_ref[...] = _rms(x_ref[...], g_ref[...]).astype(BF16)
        xn = xn_ref[...]
        tf = wg_ref.shape[1]
        hf = min(tf, MXU_COLS)
        acts = []
        for p in range(tf // hf):
            gate = jnp.dot(xn, wg_ref[:, p * hf:(p + 1) * hf], preferred_element_type=F32)
            up = jnp.dot(xn, wu_ref[:, p * hf:(p + 1) * hf], preferred_element_type=F32)
            acts.append((jax.nn.silu(gate) * up * 0.5).astype(BF16))
        acc = None
        for p, act in enumerate(acts):
            part = jnp.dot(act, wd_ref[p * hf:(p + 1) * hf, :], preferred_element_type=F32)
            acc = part if acc is None else acc + part
        o_ref[...] = (x_ref[...] if first else o_ref[...]) + acc

    pl.when(k == 0)(functools.partial(body, True))
    pl.when(k > 0)(functools.partial(body, False))


def _ffn_kernel(*refs, final_norm, n_side, adopt):
    x_ref, g_ref, wg_ref, wu_ref, wd_ref, gf_ref = refs[:6]
    n_in = 6 + adopt
    side_in = refs[n_in:n_in + n_side]
    o_ref = refs[n_in + n_side]
    side_out = refs[n_in + n_side + 1:n_in + 2 * n_side + 1]
    xn_ref = refs[n_in + 2 * n_side + 1]
    i = pl.program_id(0)
    k = pl.program_id(1)

    if adopt:
        head_ref, sem = refs[6], refs[n_in + 2 * n_side + 2]

        @pl.when((i == 0) & (k == 0))
        def _():
            cp = pltpu.make_async_copy(head_ref, o_ref, sem)
            cp.start()
            cp.wait()

        @pl.when(i > 0)
        def _():
            _ffn_step(k, x_ref, g_ref, wg_ref, wu_ref, wd_ref, o_ref, xn_ref)
    else:
        _ffn_step(k, x_ref, g_ref, wg_ref, wu_ref, wd_ref, o_ref, xn_ref)

    for src, dst in zip(side_in, side_out):
        dst[...] = src[...].astype(BF16)

    if final_norm:
        @pl.when(k == pl.num_programs(1) - 1)
        def _():
            o_ref[...] = _rms(o_ref[...], gf_ref[...])


def _ffn_head_kernel(x_ref, g_ref, wg_ref, wu_ref, wd_ref, o_ref, wgb_ref, wub_ref, wdb_ref, xn_ref):
    wgb_ref[...] = wg_ref[...].astype(BF16)
    wub_ref[...] = wu_ref[...].astype(BF16)
    wdb_ref[...] = wd_ref[...].astype(BF16)
    _ffn_step(pl.program_id(0), x_ref, g_ref, wgb_ref, wub_ref, wdb_ref, o_ref, xn_ref)


def _ffn_head(x, g, wg, wu, wd):
    tm, tf = TILES["ffn_head"]
    d = x.shape[1]
    dff = wg.shape[1]
    blocks = 2 * (tm * d * 4) * 2 + tm * d * 2 + 2 * 3 * d * tf * (4 + 2)
    temps = 3 * tm * tf * 4 + tm * d * 4
    col = lambda k: (0, k)
    row = lambda k: (k, 0)
    return pl.pallas_call(
        _ffn_head_kernel,
        grid=(dff // tf,),
        in_specs=[
            pl.BlockSpec((tm, d), lambda k: (0, 0)),
            _const_spec((1, d)),
            pl.BlockSpec((d, tf), col),
            pl.BlockSpec((d, tf), col),
            pl.BlockSpec((tf, d), row),
        ],
        out_specs=[
            pl.BlockSpec((tm, d), lambda k: (0, 0)),
            pl.BlockSpec((d, tf), col),
            pl.BlockSpec((d, tf), col),
            pl.BlockSpec((tf, d), row),
        ],
        out_shape=[
            jax.ShapeDtypeStruct((tm, d), F32),
            jax.ShapeDtypeStruct(wg.shape, BF16),
            jax.ShapeDtypeStruct(wu.shape, BF16),
            jax.ShapeDtypeStruct(wd.shape, BF16),
        ],
        scratch_shapes=[pltpu.VMEM((tm, d), BF16)],
        compiler_params=_params(blocks, temps, 1),
        name="ffn_head",
    )(x, g, wg, wu, wd)


def _cast_spec(shape, ni, nk):
    r, c = shape
    row_align = 2 * SUBLANES
    if c % (ni * LANES) == 0 and r % (nk * row_align) == 0:
        return pl.BlockSpec((r // nk, c // ni), lambda i, k: (k, i))
    m = max(m for m in range(1, nk + 1) if c % (m * LANES) == 0)
    assert r % (ni * row_align) == 0, shape
    return pl.BlockSpec((r // ni, c // m), lambda i, k: (i, jnp.minimum(k, m - 1)))


def _ffn(x, g, wg, wu, wd, gf, *, final_norm, side=(), head=None):
    tm, tf = TILES["ffn"]
    n, d = x.shape
    dff = wg.shape[1]
    ni, nk = n // tm, dff // tf
    adopt = head is not None
    if adopt:
        assert head.shape == (tm, d) and TILES["ffn_head"][0] == tm
        xi = lambda i: jnp.maximum(i, 1)
        wk = lambda i, k: jnp.where(i == 0, 0, k)
    else:
        xi = lambda i: i
        wk = lambda i, k: k
    side_specs = [_cast_spec(w.shape, ni, nk) for w in side]
    side_bytes = sum(2 * sp.block_shape[0] * sp.block_shape[1] * (4 + 2) for sp in side_specs)
    blocks = 2 * (tm * d * 4) * 2 + tm * d * 2 + 2 * 3 * d * tf * 2 + side_bytes
    temps = 3 * tm * tf * 4 + tm * d * 4
    outs = pl.pallas_call(
        functools.partial(_ffn_kernel, final_norm=final_norm, n_side=len(side), adopt=adopt),
        grid=(ni, nk),
        in_specs=[
            pl.BlockSpec((tm, d), lambda i, k: (xi(i), 0)),
            _const_spec((1, d)),
            pl.BlockSpec((d, tf), lambda i, k: (0, wk(i, k))),
            pl.BlockSpec((d, tf), lambda i, k: (0, wk(i, k))),
            pl.BlockSpec((tf, d), lambda i, k: (wk(i, k), 0)),
            _const_spec((1, d)),
        ] + ([pl.BlockSpec(memory_space=pl.ANY)] if adopt else []) + side_specs,
        out_specs=[pl.BlockSpec((tm, d), lambda i, k: (i, 0))] + side_specs,
        out_shape=[jax.ShapeDtypeStruct((n, d), F32)]
        + [jax.ShapeDtypeStruct(w.shape, BF16) for w in side],
        scratch_shapes=[pltpu.VMEM((tm, d), BF16)] + ([pltpu.SemaphoreType.DMA(())] if adopt else []),
        compiler_params=_params(blocks, temps, 2),
        name="ffn",
    )(x, g, wg, wu, wd, gf, *([head] if adopt else []), *side)
    return outs[0], list(outs[1:])


def _s5_prep_kernel(lr_ref, li_ref, ldt_ref, br_ref, bi_ref, cr_ref, ci_ref,
                    ar_ref, ai_ref, rb_ref, rc_ref, bbr_ref, bbi_ref, *, h):
    lr = lr_ref[...]
    li = li_ref[...]
    g, p = lr.shape
    dt = jnp.exp(ldt_ref[...])
    mag = jnp.exp(lr * dt)
    abar_re = mag * jnp.cos(li * dt)
    abar_im = mag * jnp.sin(li * dt)
    nr, ni = abar_re - 1.0, abar_im
    den = lr * lr + li * li
    fr = (nr * lr + ni * li) / den
    fi = (ni * lr - nr * li) / den
    ar_ref[...] = abar_re
    ai_ref[...] = abar_im
    for gg in range(g):
        rows = slice(gg * h, (gg + 1) * h)
        frg, fig = fr[gg:gg + 1, :], fi[gg:gg + 1, :]
        br, bi = br_ref[rows, :], bi_ref[rows, :]
        bbr_ref[rows, :] = frg * br - fig * bi
        bbi_ref[rows, :] = frg * bi + fig * br

    gl = LANES // h
    half = gl * p
    iota = lax.broadcasted_iota
    spread = (iota(jnp.int32, (p, half), 1) % p == iota(jnp.int32, (p, half), 0)).astype(BF16)
    spread_t = (iota(jnp.int32, (half, p), 0) % p == iota(jnp.int32, (half, p), 1)).astype(BF16)
    diag_b = iota(jnp.int32, (LANES, half), 0) // h == iota(jnp.int32, (LANES, half), 1) // p
    diag_c = iota(jnp.int32, (half, LANES), 0) // p == iota(jnp.int32, (half, LANES), 1) // h
    for j in range(g // gl):
        jj, j2 = divmod(j, 2)
        rows = slice(j * LANES, (j + 1) * LANES)
        for ri, (b_ref, c_ref, sign) in enumerate(((bbr_ref, cr_ref, 1.0), (bbi_ref, ci_ref, -1.0))):
            blk = jnp.dot(b_ref[rows, :].astype(BF16), spread, preferred_element_type=F32)
            rb_ref[jj, j2 * LANES:(j2 + 1) * LANES, ri * half:(ri + 1) * half] = (
                jnp.where(diag_b, blk, 0.0).astype(BF16))
            blk = lax.dot_general(spread_t, c_ref[rows, :].astype(BF16), (((1,), (1,)), ((), ())),
                                  preferred_element_type=F32)
            rc_ref[jj, ri * half:(ri + 1) * half, j2 * LANES:(j2 + 1) * LANES] = (
                jnp.where(diag_c, sign * blk, 0.0).astype(BF16))


def _s5_prep(a_re, a_im, log_dt, b_re, b_im, c_re, c_im):
    g, p, h = b_re.shape
    gl = LANES // h
    npair = g // (2 * gl)
    rows_gh = lambda a: a.reshape(g * h, p)
    return pl.pallas_call(
        functools.partial(_s5_prep_kernel, h=h),
        out_shape=(
            jax.ShapeDtypeStruct((g, p), F32),
            jax.ShapeDtypeStruct((g, p), F32),
            jax.ShapeDtypeStruct((npair, 2 * LANES, 2 * gl * p), BF16),
            jax.ShapeDtypeStruct((npair, 2 * gl * p, 2 * LANES), BF16),
        ),
        scratch_shapes=[pltpu.VMEM((g * h, p), F32), pltpu.VMEM((g * h, p), F32)],
        name="s5_prep",
    )(a_re, a_im, log_dt.reshape(g, 1),
      rows_gh(jnp.swapaxes(b_re, 1, 2)), rows_gh(jnp.swapaxes(b_im, 1, 2)), rows_gh(c_re), rows_gh(c_im))


def _mixer_kernel(h0_ref, hn_ref, gmix_ref, win_ref, rb_ref, rc_ref, at_ref, d_ref, wglu_ref, bglu_ref,
                  gs_ref, wp_ref, sc_ref, gp_ref, o_ref,
                  u_ref, ul_ref, s_ref, y_ref, yn_ref, st_ref, ext_ref, du_ref, xn_ref, *,
                  tc, nb, npair):
    c = pl.program_id(0)
    d = hn_ref.shape[-1]
    w = d // 2
    half = s_ref.shape[-1] // 2
    rows_per_t = 2 * nb
    nbuf = s_ref.shape[0]

    def in_proj(h_ref):
        h = h_ref[...].reshape(nb * tc, d)
        return jnp.dot(_rms(h, gmix_ref[...]).astype(BF16), win_ref[...], preferred_element_type=F32)

    @pl.when(c == 0)
    def _():
        ul_ref[...] = jnp.zeros(ul_ref.shape, F32)
        st_ref[...] = jnp.zeros(st_ref.shape, F32)
        ext_ref[:, :POOL_HALO, :] = jnp.zeros((nb, POOL_HALO, w), F32)
        u_ref[...] = in_proj(h0_ref)

    for jj in range(npair):
        for j2 in range(2):
            j = 2 * jj + j2
            for b in range(nb):
                ul_ref[jj, j2, pl.ds(b * 2 + j2, tc, stride=rows_per_t), :] = (
                    u_ref[b * tc:(b + 1) * tc, j * LANES:(j + 1) * LANES])
    du_ref[...] = d_ref[...] * u_ref[:, :w]
    for b in range(nb):
        ext_ref[b, POOL_HALO:, :] = u_ref[b * tc:(b + 1) * tc, w:]

    xn_ref[...] = _rms(hn_ref[...].reshape(nb * tc, d), gmix_ref[...]).astype(BF16)
    pw = d // npair
    gw = w // len(POOL_WINDOWS)
    t_idx = c * tc + lax.broadcasted_iota(jnp.int32, (tc, 1), 0)

    def next_u(p):
        u_ref[:, p * pw:(p + 1) * pw] = jnp.dot(
            xn_ref[...], win_ref[:, p * pw:(p + 1) * pw], preferred_element_type=F32)

    def b_proj(jj):
        lhs = jnp.concatenate([ul_ref[jj, 0], ul_ref[jj, 1]], axis=-1).astype(BF16)
        s_ref[jj % nbuf] = jnp.dot(lhs, rb_ref[jj], preferred_element_type=F32)

    def scan(jj):
        sb = jj % nbuf
        a_r = at_ref[jj, :, :half]
        a_i = at_ref[jj, :, half:]
        s_r = st_ref[jj, :, :half]
        s_i = st_ref[jj, :, half:]
        for t in range(tc):
            r0 = t * rows_per_t
            n_r = a_r * s_r - a_i * s_i + s_ref[sb, r0:r0 + rows_per_t, :half]
            n_i = a_r * s_i + a_i * s_r + s_ref[sb, r0:r0 + rows_per_t, half:]
            s_ref[sb, r0:r0 + rows_per_t, :half] = n_r
            s_ref[sb, r0:r0 + rows_per_t, half:] = n_i
            s_r, s_i = n_r, n_i
        st_ref[jj, :, :half] = s_r
        st_ref[jj, :, half:] = s_i

    def c_proj(jj):
        sb = jj % nbuf
        yy = jnp.dot(s_ref[sb].astype(BF16), rc_ref[jj], preferred_element_type=F32)
        y_ref[sb, 0] = yy[:, :LANES]
        y_ref[sb, 1] = yy[:, LANES:]
        for j2 in range(2):
            j = 2 * jj + j2
            for b in range(nb):
                yn_ref[b * tc:(b + 1) * tc, j * LANES:(j + 1) * LANES] = (
                    y_ref[sb, j2, pl.ds(b * 2 + j2, tc, stride=rows_per_t), :])

    def s5_out():
        y = yn_ref[...] + du_ref[...]
        y = jax.nn.gelu(y)
        z = jnp.dot(y.astype(BF16), wglu_ref[...], preferred_element_type=F32) + bglu_ref[...]
        y = y * jax.nn.sigmoid(z)
        o_ref[:, :, :w] = _rms(y, gs_ref[...]).astype(BF16).reshape(nb, tc, w)

    def pool(b):
        zs = []
        ssq = jnp.zeros((tc, 1), F32)
        for gi, win in enumerate(POOL_WINDOWS):
            e = ext_ref[b, :, gi * gw:(gi + 1) * gw]
            s = e
            shift = 1
            while shift < win:
                s = s + pltpu.roll(s, shift, axis=0)
                shift *= 2
            cnt = jnp.minimum(t_idx + 1, win).astype(F32)
            pooled = s[POOL_HALO:] / cnt - e[POOL_HALO:]
            z = jnp.dot(pooled.astype(BF16), wp_ref[gi], preferred_element_type=F32)
            z = z * sc_ref[:, gi * gw:(gi + 1) * gw]
            ssq = ssq + jnp.sum(z * z, axis=-1, keepdims=True)
            zs.append(z)
        inv = lax.rsqrt(ssq / w + EPS)
        for gi in range(len(POOL_WINDOWS)):
            o_ref[b, :, w + gi * gw:w + (gi + 1) * gw] = (
                zs[gi] * inv * gp_ref[:, gi * gw:(gi + 1) * gw]).astype(BF16)
        ext_ref[b, :POOL_HALO, :] = ext_ref[b, tc:, :]

    for jj in range(min(nbuf, npair)):
        b_proj(jj)
    for jj in range(npair):
        if jj < npair - 1:
            next_u(jj)
        scan(jj)
        c_proj(jj)
        if jj + nbuf < npair:
            b_proj(jj + nbuf)
        for b in range(jj * nb // npair, (jj + 1) * nb // npair):
            pool(b)
    next_u(npair - 1)
    s5_out()


def _mixer(h3, gmix, win, rb, rc, at, dskip, wglu, bglu, gs, wp, sc, gp):
    tc = TILES["mixer"]
    nb, L, d = h3.shape
    w = d // 2
    npair = rb.shape[0]
    nstate = rb.shape[2]
    rows = tc * 2 * nb
    nbuf = 2
    scratch = [
        pltpu.VMEM((nb * tc, d), F32),
        pltpu.VMEM((npair, 2, rows, LANES), F32),
        pltpu.VMEM((nbuf, rows, nstate), F32),
        pltpu.VMEM((nbuf, 2, rows, LANES), F32),
        pltpu.VMEM((nb * tc, w), F32),
        pltpu.VMEM((npair, 2 * nb, nstate), F32),
        pltpu.VMEM((nb, tc + POOL_HALO, w), F32),
        pltpu.VMEM((nb * tc, w), F32),
        pltpu.VMEM((nb * tc, d), BF16),
    ]
    blocks = (3 * nb * tc * d * 4 + 3 * nb * tc * d * 2 + d * d * 2 + rb.size * 2 + rc.size * 2
              + at.size * 4 + w * w * 2 + wp.size * 2
              + (nb * tc * d + npair * 2 * rows * LANES + nbuf * rows * nstate + nbuf * 2 * rows * LANES
                 + 2 * nb * tc * w + npair * 2 * nb * nstate + nb * (tc + POOL_HALO) * w) * 4)
    temps = nb * tc * d * 6 + rows * nstate * 6 + 4 * nb * tc * w * 4
    nchunks = L // tc
    return pl.pallas_call(
        functools.partial(_mixer_kernel, tc=tc, nb=nb, npair=npair),
        grid=(nchunks,),
        in_specs=[
            pl.BlockSpec((nb, tc, d), lambda c: (0, 0, 0), pipeline_mode=pl.Buffered(1)),
            pl.BlockSpec((nb, tc, d), lambda c: (0, jnp.minimum(c + 1, nchunks - 1), 0)),
            _const_spec((1, d)),
            _const_spec((d, d)),
            _const_spec(rb.shape),
            _const_spec(rc.shape),
            _const_spec(at.shape),
            _const_spec((1, w)),
            _const_spec((w, w)),
            _const_spec((1, w)),
            _const_spec((1, w)),
            _const_spec(wp.shape),
            _const_spec((1, w)),
            _const_spec((1, w)),
        ],
        out_specs=pl.BlockSpec((nb, tc, d), lambda c: (0, c, 0)),
        out_shape=jax.ShapeDtypeStruct((nb, L, d), BF16),
        scratch_shapes=scratch,
        compiler_params=_params(blocks, temps, 1),
        name="mixer",
    )(h3, h3, gmix, win, rb, rc, at, dskip, wglu, bglu, gs, wp, sc, gp)


def _merge_proj_kernel(h_ref, m_ref, w_ref, o_ref):
    o_ref[...] = h_ref[...] + jnp.dot(m_ref[...], w_ref[...], preferred_element_type=F32)


def _merge_proj(h, m, w):
    tm = TILES["proj"]
    n, d = h.shape
    blocks = 4 * tm * d * 4 + 2 * tm * d * 2 + d * d * 2
    temps = tm * d * 4
    return pl.pallas_call(
        _merge_proj_kernel,
        grid=(n // tm,),
        in_specs=[
            pl.BlockSpec((tm, d), lambda i: (i, 0)),
            pl.BlockSpec((tm, d), lambda i: (i, 0)),
            _const_spec((d, d)),
        ],
        out_specs=pl.BlockSpec((tm, d), lambda i: (i, 0)),
        out_shape=jax.ShapeDtypeStruct((n, d), F32),
        compiler_params=_params(blocks, temps, 1),
        name="merge_proj",
    )(h, m, w)


def _kv_kernel(m_ref, g_ref, wk_ref, wv_ref, k_ref, v_ref, mn_ref):
    @pl.when(pl.program_id(0) == 0)
    def _():
        mn_ref[...] = _rms(m_ref[...], g_ref[...]).astype(BF16)

    mn = mn_ref[...]
    k_ref[...] = jnp.dot(mn, wk_ref[...], preferred_element_type=F32).astype(BF16)
    v_ref[...] = jnp.dot(mn, wv_ref[...], preferred_element_type=F32).astype(BF16)


def _kv(mem2, g, wk, wv):
    tn = TILES["kv"]
    n, d = mem2.shape
    blocks = n * d * 4 + n * d * 2 + 4 * d * tn * 2 + 4 * n * tn * 2
    temps = 2 * n * tn * 4 + n * d * 4
    return pl.pallas_call(
        _kv_kernel,
        grid=(d // tn,),
        in_specs=[
            _const_spec((n, d)),
            _const_spec((1, d)),
            pl.BlockSpec((d, tn), lambda j: (0, j)),
            pl.BlockSpec((d, tn), lambda j: (0, j)),
        ],
        out_specs=(pl.BlockSpec((n, tn), lambda j: (0, j)), pl.BlockSpec((n, tn), lambda j: (0, j))),
        out_shape=(jax.ShapeDtypeStruct((n, d), BF16), jax.ShapeDtypeStruct((n, d), BF16)),
        scratch_shapes=[pltpu.VMEM((n, d), BF16)],
        compiler_params=_params(blocks, temps, 1),
        name="mem_kv",
    )(mem2, g, wk, wv)


def _xattn_kernel(h_ref, g_ref, wq_ref, k_ref, v_ref, wo_ref, o_ref):
    h = h_ref[...]
    d = h.shape[-1]
    hd = d // MEM_HEADS
    q = jnp.dot(_rms(h, g_ref[...]).astype(BF16), wq_ref[...], preferred_element_type=F32)
    outs = []
    for hh in range(MEM_HEADS):
        qh = q[:, hh * hd:(hh + 1) * hd].astype(BF16)
        kh = k_ref[:, hh * hd:(hh + 1) * hd]
        s = lax.dot_general(qh, kh, (((1,), (1,)), ((), ())), preferred_element_type=F32)
        s = s * (hd ** -0.5)
        e = jnp.exp(s - jnp.max(s, axis=-1, keepdims=True))
        p = e / jnp.sum(e, axis=-1, keepdims=True)
        outs.append(jnp.dot(p.astype(BF16), v_ref[:, hh * hd:(hh + 1) * hd],
                            preferred_element_type=F32).astype(BF16))
    o = jnp.concatenate(outs, axis=-1)
    o_ref[...] = h + jnp.dot(o, wo_ref[...], preferred_element_type=F32)


def _xattn(h, g, wq, k, v, wo, *, nb):
    tm = TILES["xattn"]
    n, d = h.shape
    nm = k.shape[0] // nb
    tiles = n // nb // tm
    blocks = 4 * tm * d * 4 + 2 * d * d * 2 + 4 * nm * d * 2
    temps = 4 * tm * d * 4
    return pl.pallas_call(
        _xattn_kernel,
        grid=(nb, tiles),
        in_specs=[
            pl.BlockSpec((tm, d), lambda b, i: (b * tiles + i, 0)),
            _const_spec((1, d)),
            _const_spec((d, d)),
            pl.BlockSpec((nm, d), lambda b, i: (b, 0)),
            pl.BlockSpec((nm, d), lambda b, i: (b, 0)),
            _const_spec((d, d)),
        ],
        out_specs=pl.BlockSpec((tm, d), lambda b, i: (b * tiles + i, 0)),
        out_shape=jax.ShapeDtypeStruct((n, d), F32),
        compiler_params=_params(blocks, temps, 2),
        name="xattn",
    )(h, g, wq, k, v, wo)


def _s5_pole_table(abar_re, abar_im, npair, nb):
    g, p = abar_re.shape
    gl = g // (2 * npair)
    ab = jnp.stack([abar_re, abar_im]).reshape(2, npair, 2, gl, p)
    at = jnp.transpose(ab, (1, 2, 0, 3, 4)).reshape(npair, 1, 2, 2 * gl * p)
    return jnp.broadcast_to(at, (npair, nb, 2, 2 * gl * p)).reshape(npair, 2 * nb, 2 * gl * p)


def kernel(x, mem, g_ffn1, w1_gate, w1_up, w1_down, g_mix, w_in, ssm_a_re, ssm_a_im, ssm_log_dt,
           ssm_b_re, ssm_b_im, ssm_c_re, ssm_c_im, ssm_d, w_glu, b_glu, w_pool, pool_scale,
           g_out_ssm, g_out_pool, w_out, g_xattn, g_mem, w_q, w_k, w_v, w_o,
           g_ffn2, w2_gate, w2_up, w2_down, g_final):
    nb, L, d = x.shape
    n = nb * L
    depth = g_ffn1.shape[0]
    bf = lambda a: a.astype(BF16)
    row = lambda a: a.reshape(1, -1)

    h = x.reshape(n, d)
    for l in range(depth):
        last = l == depth - 1
        pool_shape = w_pool[l].shape
        later = [w2_gate[l], w2_up[l], w2_down[l], w_in[l], w_out[l], w_q[l], w_k[l], w_v[l], w_o[l],
                 w_glu[l], w_pool[l].reshape(-1, pool_shape[-1])]
        head, w1g, w1u, w1d = _ffn_head(h, row(g_ffn1[l]), w1_gate[l], w1_up[l], w1_down[l])
        h, later = _ffn(h, row(g_ffn1[l]), w1g, w1u, w1d, row(g_final),
                        final_norm=False, side=later, head=head)
        w2g, w2u, w2d, win, wout, wq, wk, wv, wo, wglu, wpool = later

        abar_re, abar_im, rb, rc = _s5_prep(ssm_a_re[l], ssm_a_im[l], ssm_log_dt[l],
                                            ssm_b_re[l], ssm_b_im[l], ssm_c_re[l], ssm_c_im[l])
        at = _s5_pole_table(abar_re, abar_im, rb.shape[0], nb)
        merged = _mixer(h.reshape(nb, L, d), row(g_mix[l]), win, rb, rc, at, row(ssm_d[l]),
                        wglu, row(b_glu[l]), row(g_out_ssm[l]),
                        wpool.reshape(pool_shape), row(pool_scale[l]), row(g_out_pool[l]))
        h = _merge_proj(h, merged.reshape(n, d), wout)

        k, v = _kv(mem.reshape(nb * mem.shape[1], d), row(g_mem[l]), wk, wv)
        h = _xattn(h, row(g_xattn[l]), wq, k, v, wo, nb=nb)

        h, _ = _ffn(h, row(g_ffn2[l]), w2g, w2u, w2d, row(g_final), final_norm=last)
    return h.reshape(nb, L, d)
```

```python
import functools

import jax
import jax.numpy as jnp
from jax import lax
from jax.experimental import pallas as pl
from jax.experimental.pallas import tpu as pltpu

F32 = jnp.float32
BF16 = jnp.bfloat16

EPS = 1e-6
POOL_WINDOWS = (2, 4, 8, 16)
MEM_HEADS = 4

LANES = 128
SUBLANES = 8
MXU_COLS = 256
VMEM_BYTES_V7X = 64 * 1024 * 1024
VMEM_RESERVE = 2 * 1024 * 1024
POOL_HALO = 16

TILES = dict(ffn=(1024, 512), ffn_head=(1024, 256), mixer=128, xattn=512, kv=512)


def _params(block_bytes, temp_bytes, ndims):
    return pltpu.CompilerParams(
        dimension_semantics=("arbitrary",) * ndims,
        vmem_limit_bytes=int(min(VMEM_BYTES_V7X - VMEM_RESERVE, block_bytes + temp_bytes)),
    )


def _rms(x, g):
    return x * lax.rsqrt(jnp.mean(x * x, axis=-1, keepdims=True) + EPS) * g


def _const_spec(shape):
    nd = len(shape)
    return pl.BlockSpec(shape, lambda *_: (0,) * nd, pipeline_mode=pl.Buffered(1))


def _ffn_step(k, x_ref, g_ref, wg_ref, wu_ref, wd_ref, o_ref, xn_ref):
    def body(first):
        if first:
            xn_ref[...] = _rms(x_ref[...], g_ref[...]).astype(BF16)
        xn = xn_ref[...]
        tf = wg_ref.shape[1]
        hf = min(tf, MXU_COLS)
        acts = []
        for p in range(tf // hf):
            gate = jnp.dot(xn, wg_ref[:, p * hf:(p + 1) * hf], preferred_element_type=F32)
            up = jnp.dot(xn, wu_ref[:, p * hf:(p + 1) * hf], preferred_element_type=F32)
            acts.append((jax.nn.silu(gate) * up * 0.5).astype(BF16))
        acc = None
        for p, act in enumerate(acts):
            part = jnp.dot(act, wd_ref[p * hf:(p + 1) * hf, :], preferred_element_type=F32)
            acc = part if acc is None else acc + part
        o_ref[...] = (x_ref[...] if first else o_ref[...]) + acc

    pl.when(k == 0)(functools.partial(body, True))
    pl.when(k > 0)(functools.partial(body, False))


def _ffn_kernel(*refs, final_norm, n_side, adopt):
    x_ref, g_ref, wg_ref, wu_ref, wd_ref, gf_ref = refs[:6]
    n_in = 6 + adopt
    side_in = refs[n_in:n_in + n_side]
    o_ref = refs[n_in + n_side]
    side_out = refs[n_in + n_side + 1:n_in + 2 * n_side + 1]
    xn_ref = refs[n_in + 2 * n_side + 1]
    i = pl.program_id(0)
    k = pl.program_id(1)

    if adopt:
        head_ref, sem = refs[6], refs[n_in + 2 * n_side + 2]

        @pl.when((i == 0) & (k == 0))
        def _():
            cp = pltpu.make_async_copy(head_ref, o_ref, sem)
            cp.start()
            cp.wait()

        @pl.when(i > 0)
        def _():
            _ffn_step(k, x_ref, g_ref, wg_ref, wu_ref, wd_ref, o_ref, xn_ref)
    else:
        _ffn_step(k, x_ref, g_ref, wg_ref, wu_ref, wd_ref, o_ref, xn_ref)

    for src, dst in zip(side_in, side_out):
        dst[...] = src[...].astype(BF16)

    if final_norm:
        @pl.when(k == pl.num_programs(1) - 1)
        def _():
            o_ref[...] = _rms(o_ref[...], gf_ref[...])


def _ffn_head_kernel(x_ref, g_ref, wg_ref, wu_ref, wd_ref, o_ref, wgb_ref, wub_ref, wdb_ref, xn_ref):
    wgb_ref[...] = wg_ref[...].astype(BF16)
    wub_ref[...] = wu_ref[...].astype(BF16)
    wdb_ref[...] = wd_ref[...].astype(BF16)
    _ffn_step(pl.program_id(0), x_ref, g_ref, wgb_ref, wub_ref, wdb_ref, o_ref, xn_ref)


def _ffn_head(x, g, wg, wu, wd):
    tm, tf = TILES["ffn_head"]
    d = x.shape[1]
    dff = wg.shape[1]
    blocks = 2 * (tm * d * 4) * 2 + tm * d * 2 + 2 * 3 * d * tf * (4 + 2)
    temps = 3 * tm * tf * 4 + tm * d * 4
    col = lambda k: (0, k)
    row = lambda k: (k, 0)
    return pl.pallas_call(
        _ffn_head_kernel,
        grid=(dff // tf,),
        in_specs=[
            pl.BlockSpec((tm, d), lambda k: (0, 0)),
            _const_spec((1, d)),
            pl.BlockSpec((d, tf), col),
            pl.BlockSpec((d, tf), col),
            pl.BlockSpec((tf, d), row),
        ],
        out_specs=[
            pl.BlockSpec((tm, d), lambda k: (0, 0)),
            pl.BlockSpec((d, tf), col),
            pl.BlockSpec((d, tf), col),
            pl.BlockSpec((tf, d), row),
        ],
        out_shape=[
            jax.ShapeDtypeStruct((tm, d), F32),
            jax.ShapeDtypeStruct(wg.shape, BF16),
            jax.ShapeDtypeStruct(wu.shape, BF16),
            jax.ShapeDtypeStruct(wd.shape, BF16),
        ],
        scratch_shapes=[pltpu.VMEM((tm, d), BF16)],
        compiler_params=_params(blocks, temps, 1),
        name="ffn_head",
    )(x, g, wg, wu, wd)


def _cast_spec(shape, ni, nk):
    r, c = shape
    row_align = 2 * SUBLANES
    if c % (ni * LANES) == 0 and r % (nk * row_align) == 0:
        return pl.BlockSpec((r // nk, c // ni), lambda i, k: (k, i))
    m = max(m for m in range(1, nk + 1) if c % (m * LANES) == 0)
    assert r % (ni * row_align) == 0, shape
    return pl.BlockSpec((r // ni, c // m), lambda i, k: (i, jnp.minimum(k, m - 1)))


def _ffn(x, g, wg, wu, wd, gf, *, final_norm, side=(), head=None):
    tm, tf = TILES["ffn"]
    n, d = x.shape
    dff = wg.shape[1]
    ni, nk = n // tm, dff // tf
    adopt = head is not None
    if adopt:
        assert head.shape == (tm, d) and TILES["ffn_head"][0] == tm
        xi = lambda i: jnp.maximum(i, 1)
        wk = lambda i, k: jnp.where(i == 0, 0, k)
    else:
        xi = lambda i: i
        wk = lambda i, k: k
    side_specs = [_cast_spec(w.shape, ni, nk) for w in side]
    side_bytes = sum(2 * sp.block_shape[0] * sp.block_shape[1] * (4 + 2) for sp in side_specs)
    blocks = 2 * (tm * d * 4) * 2 + tm * d * 2 + 2 * 3 * d * tf * 2 + side_bytes
    temps = 3 * tm * tf * 4 + tm * d * 4
    outs = pl.pallas_call(
        functools.partial(_ffn_kernel, final_norm=final_norm, n_side=len(side), adopt=adopt),
        grid=(ni, nk),
        in_specs=[
            pl.BlockSpec((tm, d), lambda i, k: (xi(i), 0)),
            _const_spec((1, d)),
            pl.BlockSpec((d, tf), lambda i, k: (0, wk(i, k))),
            pl.BlockSpec((d, tf), lambda i, k: (0, wk(i, k))),
            pl.BlockSpec((tf, d), lambda i, k: (wk(i, k), 0)),
            _const_spec((1, d)),
        ] + ([pl.BlockSpec(memory_space=pl.ANY)] if adopt else []) + side_specs,
        out_specs=[pl.BlockSpec((tm, d), lambda i, k: (i, 0))] + side_specs,
        out_shape=[jax.ShapeDtypeStruct((n, d), F32)]
        + [jax.ShapeDtypeStruct(w.shape, BF16) for w in side],
        scratch_shapes=[pltpu.VMEM((tm, d), BF16)] + ([pltpu.SemaphoreType.DMA(())] if adopt else []),
        compiler_params=_params(blocks, temps, 2),
        name="ffn",
    )(x, g, wg, wu, wd, gf, *([head] if adopt else []), *side)
    return outs[0], list(outs[1:])


def _s5_prep_kernel(lr_ref, li_ref, ldt_ref, br_ref, bi_ref, cr_ref, ci_ref,
                    ar_ref, ai_ref, rb_ref, rc_ref, bbr_ref, bbi_ref, *, h):
    lr = lr_ref[...]
    li = li_ref[...]
    g, p = lr.shape
    dt = jnp.exp(ldt_ref[...])
    mag = jnp.exp(lr * dt)
    abar_re = mag * jnp.cos(li * dt)
    abar_im = mag * jnp.sin(li * dt)
    nr, ni = abar_re - 1.0, abar_im
    den = lr * lr + li * li
    fr = (nr * lr + ni * li) / den
    fi = (ni * lr - nr * li) / den
    ar_ref[...] = abar_re
    ai_ref[...] = abar_im
    for gg in range(g):
        rows = slice(gg * h, (gg + 1) * h)
        frg, fig = fr[gg:gg + 1, :], fi[gg:gg + 1, :]
        br, bi = br_ref[rows, :], bi_ref[rows, :]
        bbr_ref[rows, :] = frg * br - fig * bi
        bbi_ref[rows, :] = frg * bi + fig * br

    gl = LANES // h
    half = gl * p
    iota = lax.broadcasted_iota
    spread = (iota(jnp.int32, (p, half), 1) % p == iota(jnp.int32, (p, half), 0)).astype(BF16)
    spread_t = (iota(jnp.int32, (half, p), 0) % p == iota(jnp.int32, (half, p), 1)).astype(BF16)
    diag_b = iota(jnp.int32, (LANES, half), 0) // h == iota(jnp.int32, (LANES, half), 1) // p
    diag_c = iota(jnp.int32, (half, LANES), 0) // p == iota(jnp.int32, (half, LANES), 1) // h
    for j in range(g // gl):
        jj, j2 = divmod(j, 2)
        rows = slice(j * LANES, (j + 1) * LANES)
        for ri, (b_ref, c_ref, sign) in enumerate(((bbr_ref, cr_ref, 1.0), (bbi_ref, ci_ref, -1.0))):
            blk = jnp.dot(b_ref[rows, :].astype(BF16), spread, preferred_element_type=F32)
            rb_ref[jj, j2 * LANES:(j2 + 1) * LANES, ri * half:(ri + 1) * half] = (
                jnp.where(diag_b, blk, 0.0).astype(BF16))
            blk = lax.dot_general(spread_t, c_ref[rows, :].astype(BF16), (((1,), (1,)), ((), ())),
                                  preferred_element_type=F32)
            rc_ref[jj, ri * half:(ri + 1) * half, j2 * LANES:(j2 + 1) * LANES] = (
                jnp.where(diag_c, sign * blk, 0.0).astype(BF16))


def _s5_prep(a_re, a_im, log_dt, b_re, b_im, c_re, c_im):
    g, p, h = b_re.shape
    gl = LANES // h
    npair = g // (2 * gl)
    rows_gh = lambda a: a.reshape(g * h, p)
    return pl.pallas_call(
        functools.partial(_s5_prep_kernel, h=h),
        out_shape=(
            jax.ShapeDtypeStruct((g, p), F32),
            jax.ShapeDtypeStruct((g, p), F32),
            jax.ShapeDtypeStruct((npair, 2 * LANES, 2 * gl * p), BF16),
            jax.ShapeDtypeStruct((npair, 2 * gl * p, 2 * LANES), BF16),
        ),
        scratch_shapes=[pltpu.VMEM((g * h, p), F32), pltpu.VMEM((g * h, p), F32)],
        name="s5_prep",
    )(a_re, a_im, log_dt.reshape(g, 1),
      rows_gh(jnp.swapaxes(b_re, 1, 2)), rows_gh(jnp.swapaxes(b_im, 1, 2)), rows_gh(c_re), rows_gh(c_im))


def _mixer_kernel(h0_ref, hn_ref, gmix_ref, win_ref, rb_ref, rc_ref, at_ref, d_ref, wglu_ref, bglu_ref,
                  gs_ref, wp_ref, sc_ref, gp_ref, o_ref,
                  u_ref, ul_ref, s_ref, y_ref, yn_ref, st_ref, ext_ref, du_ref, xn_ref, *,
                  tc, nb, npair):
    c = pl.program_id(0)
    d = hn_ref.shape[-1]
    w = d // 2
    half = s_ref.shape[-1] // 2
    rows_per_t = 2 * nb
    nbuf = s_ref.shape[0]

    def in_proj(h_ref):
        h = h_ref[...].reshape(nb * tc, d)
        return jnp.dot(_rms(h, gmix_ref[...]).astype(BF16), win_ref[...], preferred_element_type=F32)

    @pl.when(c == 0)
    def _():
        ul_ref[...] = jnp.zeros(ul_ref.shape, F32)
        st_ref[...] = jnp.zeros(st_ref.shape, F32)
        ext_ref[:, :POOL_HALO, :] = jnp.zeros((nb, POOL_HALO, w), F32)
        u_ref[...] = in_proj(h0_ref)

    for jj in range(npair):
        for j2 in range(2):
            j = 2 * jj + j2
            for b in range(nb):
                ul_ref[jj, j2, pl.ds(b * 2 + j2, tc, stride=rows_per_t), :] = (
                    u_ref[b * tc:(b + 1) * tc, j * LANES:(j + 1) * LANES])
    du_ref[...] = d_ref[...] * u_ref[:, :w]
    for b in range(nb):
        ext_ref[b, POOL_HALO:, :] = u_ref[b * tc:(b + 1) * tc, w:]

    xn_ref[...] = _rms(hn_ref[...].reshape(nb * tc, d), gmix_ref[...]).astype(BF16)
    pw = d // npair
    gw = w // len(POOL_WINDOWS)
    t_idx = c * tc + lax.broadcasted_iota(jnp.int32, (tc, 1), 0)

    def next_u(p):
        u_ref[:, p * pw:(p + 1) * pw] = jnp.dot(
            xn_ref[...], win_ref[:, p * pw:(p + 1) * pw], preferred_element_type=F32)

    def b_proj(jj):
        lhs = jnp.concatenate([ul_ref[jj, 0], ul_ref[jj, 1]], axis=-1).astype(BF16)
        s_ref[jj % nbuf] = jnp.dot(lhs, rb_ref[jj], preferred_element_type=F32)

    def scan(jj):
        sb = jj % nbuf
        a_r = at_ref[jj, :, :half]
        a_i = at_ref[jj, :, half:]
        s_r = st_ref[jj, :, :half]
        s_i = st_ref[jj, :, half:]
        for t in range(tc):
            r0 = t * rows_per_t
            n_r = a_r * s_r - a_i * s_i + s_ref[sb, r0:r0 + rows_per_t, :half]
            n_i = a_r * s_i + a_i * s_r + s_ref[sb, r0:r0 + rows_per_t, half:]
            s_ref[sb, r0:r0 + rows_per_t, :half] = n_r
            s_ref[sb, r0:r0 + rows_per_t, half:] = n_i
            s_r, s_i = n_r, n_i
        st_ref[jj, :, :half] = s_r
        st_ref[jj, :, half:] = s_i

    def c_proj(jj):
        sb = jj % nbuf
        yy = jnp.dot(s_ref[sb].astype(BF16), rc_ref[jj], preferred_element_type=F32)
        y_ref[sb, 0] = yy[:, :LANES]
        y_ref[sb, 1] = yy[:, LANES:]
        for j2 in range(2):
            j = 2 * jj + j2
            for b in range(nb):
                yn_ref[b * tc:(b + 1) * tc, j * LANES:(j + 1) * LANES] = (
                    y_ref[sb, j2, pl.ds(b * 2 + j2, tc, stride=rows_per_t), :])

    def s5_out():
        y = yn_ref[...] + du_ref[...]
        y = jax.nn.gelu(y)
        z = jnp.dot(y.astype(BF16), wglu_ref[...], preferred_element_type=F32) + bglu_ref[...]
        y = y * jax.nn.sigmoid(z)
        o_ref[:, :, :w] = _rms(y, gs_ref[...]).astype(BF16).reshape(nb, tc, w)

    def pool(b):
        zs = []
        ssq = jnp.zeros((tc, 1), F32)
        for gi, win in enumerate(POOL_WINDOWS):
            e = ext_ref[b, :, gi * gw:(gi + 1) * gw]
            s = e
            shift = 1
            while shift < win:
                s = s + pltpu.roll(s, shift, axis=0)
                shift *= 2
            cnt = jnp.minimum(t_idx + 1, win).astype(F32)
            pooled = s[POOL_HALO:] / cnt - e[POOL_HALO:]
            z = jnp.dot(pooled.astype(BF16), wp_ref[gi], preferred_element_type=F32)
            z = z * sc_ref[:, gi * gw:(gi + 1) * gw]
            ssq = ssq + jnp.sum(z * z, axis=-1, keepdims=True)
            zs.append(z)
        inv = lax.rsqrt(ssq / w + EPS)
        for gi in range(len(POOL_WINDOWS)):
            o_ref[b, :, w + gi * gw:w + (gi + 1) * gw] = (
                zs[gi] * inv * gp_ref[:, gi * gw:(gi + 1) * gw]).astype(BF16)
        ext_ref[b, :POOL_HALO, :] = ext_ref[b, tc:, :]

    for jj in range(min(nbuf, npair)):
        b_proj(jj)
    for jj in range(npair):
        if jj < npair - 1:
            next_u(jj)
        scan(jj)
        c_proj(jj)
        if jj + nbuf < npair:
            b_proj(jj + nbuf)
        for b in range(jj * nb // npair, (jj + 1) * nb // npair):
            pool(b)
    next_u(npair - 1)
    s5_out()


def _mixer(h3, gmix, win, rb, rc, at, dskip, wglu, bglu, gs, wp, sc, gp):
    tc = TILES["mixer"]
    nb, L, d = h3.shape
    w = d // 2
    npair = rb.shape[0]
    nstate = rb.shape[2]
    rows = tc * 2 * nb
    nbuf = 2
    scratch = [
        pltpu.VMEM((nb * tc, d), F32),
        pltpu.VMEM((npair, 2, rows, LANES), F32),
        pltpu.VMEM((nbuf, rows, nstate), F32),
        pltpu.VMEM((nbuf, 2, rows, LANES), F32),
        pltpu.VMEM((nb * tc, w), F32),
        pltpu.VMEM((npair, 2 * nb, nstate), F32),
        pltpu.VMEM((nb, tc + POOL_HALO, w), F32),
        pltpu.VMEM((nb * tc, w), F32),
        pltpu.VMEM((nb * tc, d), BF16),
    ]
    blocks = (3 * nb * tc * d * 4 + 3 * nb * tc * d * 2 + d * d * 2 + rb.size * 2 + rc.size * 2
              + at.size * 4 + w * w * 2 + wp.size * 2
              + (nb * tc * d + npair * 2 * rows * LANES + nbuf * rows * nstate + nbuf * 2 * rows * LANES
                 + 2 * nb * tc * w + npair * 2 * nb * nstate + nb * (tc + POOL_HALO) * w) * 4)
    temps = nb * tc * d * 6 + rows * nstate * 6 + 4 * nb * tc * w * 4
    nchunks = L // tc
    return pl.pallas_call(
        functools.partial(_mixer_kernel, tc=tc, nb=nb, npair=npair),
        grid=(nchunks,),
        in_specs=[
            pl.BlockSpec((nb, tc, d), lambda c: (0, 0, 0), pipeline_mode=pl.Buffered(1)),
            pl.BlockSpec((nb, tc, d), lambda c: (0, jnp.minimum(c + 1, nchunks - 1), 0)),
            _const_spec((1, d)),
            _const_spec((d, d)),
            _const_spec(rb.shape),
            _const_spec(rc.shape),
            _const_spec(at.shape),
            _const_spec((1, w)),
            _const_spec((w, w)),
            _const_spec((1, w)),
            _const_spec((1, w)),
            _const_spec(wp.shape),
            _const_spec((1, w)),
            _const_spec((1, w)),
        ],
        out_specs=pl.BlockSpec((nb, tc, d), lambda c: (0, c, 0)),
        out_shape=jax.ShapeDtypeStruct((nb, L, d), BF16),
        scratch_shapes=scratch,
        compiler_params=_params(blocks, temps, 1),
        name="mixer",
    )(h3, h3, gmix, win, rb, rc, at, dskip, wglu, bglu, gs, wp, sc, gp)


def _kv_kernel(m_ref, g_ref, wk_ref, wv_ref, k_ref, v_ref, mn_ref):
    @pl.when(pl.program_id(0) == 0)
    def _():
        mn_ref[...] = _rms(m_ref[...], g_ref[...]).astype(BF16)

    mn = mn_ref[...]
    k_ref[...] = jnp.dot(mn, wk_ref[...], preferred_element_type=F32).astype(BF16)
    v_ref[...] = jnp.dot(mn, wv_ref[...], preferred_element_type=F32).astype(BF16)


def _kv(mem2, g, wk, wv):
    tn = TILES["kv"]
    n, d = mem2.shape
    blocks = n * d * 4 + n * d * 2 + 4 * d * tn * 2 + 4 * n * tn * 2
    temps = 2 * n * tn * 4 + n * d * 4
    return pl.pallas_call(
        _kv_kernel,
        grid=(d // tn,),
        in_specs=[
            _const_spec((n, d)),
            _const_spec((1, d)),
            pl.BlockSpec((d, tn), lambda j: (0, j)),
            pl.BlockSpec((d, tn), lambda j: (0, j)),
        ],
        out_specs=(pl.BlockSpec((n, tn), lambda j: (0, j)), pl.BlockSpec((n, tn), lambda j: (0, j))),
        out_shape=(jax.ShapeDtypeStruct((n, d), BF16), jax.ShapeDtypeStruct((n, d), BF16)),
        scratch_shapes=[pltpu.VMEM((n, d), BF16)],
        compiler_params=_params(blocks, temps, 1),
        name="mem_kv",
    )(mem2, g, wk, wv)


def _xattn_kernel(h_ref, m_ref, wout_ref, g_ref, wq_ref, k_ref, v_ref, wo_ref, o_ref):
    o_ref[...] = h_ref[...] + jnp.dot(m_ref[...], wout_ref[...], preferred_element_type=F32)
    h = o_ref[...]
    d = h.shape[-1]
    hd = d // MEM_HEADS
    q = jnp.dot(_rms(h, g_ref[...]).astype(BF16), wq_ref[...], preferred_element_type=F32)
    outs = []
    for hh in range(MEM_HEADS):
        qh = q[:, hh * hd:(hh + 1) * hd].astype(BF16)
        kh = k_ref[:, hh * hd:(hh + 1) * hd]
        s = lax.dot_general(qh, kh, (((1,), (1,)), ((), ())), preferred_element_type=F32)
        s = s * (hd ** -0.5)
        e = jnp.exp(s - jnp.max(s, axis=-1, keepdims=True))
        p = e / jnp.sum(e, axis=-1, keepdims=True)
        outs.append(jnp.dot(p.astype(BF16), v_ref[:, hh * hd:(hh + 1) * hd],
                            preferred_element_type=F32).astype(BF16))
    o = jnp.concatenate(outs, axis=-1)
    o_ref[...] = h + jnp.dot(o, wo_ref[...], preferred_element_type=F32)


def _xattn(h, m, wout, g, wq, k, v, wo, *, nb):
    tm = TILES["xattn"]
    n, d = h.shape
    nm = k.shape[0] // nb
    tiles = n // nb // tm
    rows = lambda b, i: (b * tiles + i, 0)
    blocks = 4 * tm * d * 4 + 2 * tm * d * 2 + 3 * d * d * 2 + 4 * nm * d * 2
    temps = 3 * tm * d * 4
    return pl.pallas_call(
        _xattn_kernel,
        grid=(nb, tiles),
        in_specs=[
            pl.BlockSpec((tm, d), rows),
            pl.BlockSpec((tm, d), rows),
            _const_spec((d, d)),
            _const_spec((1, d)),
            _const_spec((d, d)),
            pl.BlockSpec((nm, d), lambda b, i: (b, 0)),
            pl.BlockSpec((nm, d), lambda b, i: (b, 0)),
            _const_spec((d, d)),
        ],
        out_specs=pl.BlockSpec((tm, d), rows),
        out_shape=jax.ShapeDtypeStruct((n, d), F32),
        compiler_params=_params(blocks, temps, 2),
        name="xattn",
    )(h, m, wout, g, wq, k, v, wo)


def _s5_pole_table(abar_re, abar_im, npair, nb):
    g, p = abar_re.shape
    gl = g // (2 * npair)
    ab = jnp.stack([abar_re, abar_im]).reshape(2, npair, 2, gl, p)
    at = jnp.transpose(ab, (1, 2, 0, 3, 4)).reshape(npair, 1, 2, 2 * gl * p)
    return jnp.broadcast_to(at, (npair, nb, 2, 2 * gl * p)).reshape(npair, 2 * nb, 2 * gl * p)


def kernel(x, mem, g_ffn1, w1_gate, w1_up, w1_down, g_mix, w_in, ssm_a_re, ssm_a_im, ssm_log_dt,
           ssm_b_re, ssm_b_im, ssm_c_re, ssm_c_im, ssm_d, w_glu, b_glu, w_pool, pool_scale,
           g_out_ssm, g_out_pool, w_out, g_xattn, g_mem, w_q, w_k, w_v, w_o,
           g_ffn2, w2_gate, w2_up, w2_down, g_final):
    nb, L, d = x.shape
    n = nb * L
    depth = g_ffn1.shape[0]
    bf = lambda a: a.astype(BF16)
    row = lambda a: a.reshape(1, -1)

    h = x.reshape(n, d)
    for l in range(depth):
        last = l == depth - 1
        pool_shape = w_pool[l].shape
        later = [w2_gate[l], w2_up[l], w2_down[l], w_in[l], w_out[l], w_q[l], w_k[l], w_v[l], w_o[l],
                 w_glu[l], w_pool[l].reshape(-1, pool_shape[-1])]
        head, w1g, w1u, w1d = _ffn_head(h, row(g_ffn1[l]), w1_gate[l], w1_up[l], w1_down[l])
        h, later = _ffn(h, row(g_ffn1[l]), w1g, w1u, w1d, row(g_final),
                        final_norm=False, side=later, head=head)
        w2g, w2u, w2d, win, wout, wq, wk, wv, wo, wglu, wpool = later

        abar_re, abar_im, rb, rc = _s5_prep(ssm_a_re[l], ssm_a_im[l], ssm_log_dt[l],
                                            ssm_b_re[l], ssm_b_im[l], ssm_c_re[l], ssm_c_im[l])
        at = _s5_pole_table(abar_re, abar_im, rb.shape[0], nb)
        merged = _mixer(h.reshape(nb, L, d), row(g_mix[l]), win, rb, rc, at, row(ssm_d[l]),
                        wglu, row(b_glu[l]), row(g_out_ssm[l]),
                        wpool.reshape(pool_shape), row(pool_scale[l]), row(g_out_pool[l]))
        k, v = _kv(mem.reshape(nb * mem.shape[1], d), row(g_mem[l]), wk, wv)
        h = _xattn(h, merged.reshape(n, d), wout, row(g_xattn[l]), wq, k, v, wo, nb=nb)

        h, _ = _ffn(h, row(g_ffn2[l]), w2g, w2u, w2d, row(g_final), final_norm=last)
    return h.reshape(nb, L, d)
```

```python
import functools

import jax
import jax.numpy as jnp
from jax import lax
from jax.experimental import pallas as pl
from jax.experimental.pallas import tpu as pltpu

F32 = jnp.float32
BF16 = jnp.bfloat16

EPS = 1e-6
POOL_WINDOWS = (2, 4, 8, 16)
MEM_HEADS = 4

LANES = 128
SUBLANES = 8
MXU_COLS = 256
VMEM_BYTES_V7X = 64 * 1024 * 1024
VMEM_RESERVE = 2 * 1024 * 1024
POOL_HALO = 16

TILES = dict(ffn=(1024, 512), ffn_head=(1024, 256), mixer=128, xattn=512, kv=512)


def _params(block_bytes, temp_bytes, ndims):
    return pltpu.CompilerParams(
        dimension_semantics=("arbitrary",) * ndims,
        vmem_limit_bytes=int(min(VMEM_BYTES_V7X - VMEM_RESERVE, block_bytes + temp_bytes)),
    )


def _rms(x, g):
    return x * lax.rsqrt(jnp.mean(x * x, axis=-1, keepdims=True) + EPS) * g


def _const_spec(shape):
    nd = len(shape)
    return pl.BlockSpec(shape, lambda *_: (0,) * nd, pipeline_mode=pl.Buffered(1))


def _ffn_step(k, x_ref, g_ref, wg_ref, wu_ref, wd_ref, o_ref, xn_ref):
    def body(first):
        if first:
            xn_ref[...] = _rms(x_ref[...], g_ref[...]).astype(BF16)
        xn = xn_ref[...]
        tf = wg_ref.shape[1]
        hf = min(tf, MXU_COLS)
        acts = []
        for p in range(tf // hf):
            gate = jnp.dot(xn, wg_ref[:, p * hf:(p + 1) * hf], preferred_element_type=F32)
            up = jnp.dot(xn, wu_ref[:, p * hf:(p + 1) * hf], preferred_element_type=F32)
            acts.append((jax.nn.silu(gate) * up * 0.5).astype(BF16))
        acc = None
        for p, act in enumerate(acts):
            part = jnp.dot(act, wd_ref[p * hf:(p + 1) * hf, :], preferred_element_type=F32)
            acc = part if acc is None else acc + part
        o_ref[...] = (x_ref[...] if first else o_ref[...]) + acc

    pl.when(k == 0)(functools.partial(body, True))
    pl.when(k > 0)(functools.partial(body, False))


def _ffn_kernel(*refs, final_norm, n_side, adopt):
    x_ref, g_ref, wg_ref, wu_ref, wd_ref, gf_ref = refs[:6]
    n_in = 6 + adopt
    side_in = refs[n_in:n_in + n_side]
    o_ref = refs[n_in + n_side]
    side_out = refs[n_in + n_side + 1:n_in + 2 * n_side + 1]
    xn_ref = refs[n_in + 2 * n_side + 1]
    i = pl.program_id(0)
    k = pl.program_id(1)

    if adopt:
        head_ref, sem = refs[6], refs[n_in + 2 * n_side + 2]

        @pl.when((i == 0) & (k == 0))
        def _():
            cp = pltpu.make_async_copy(head_ref, o_ref, sem)
            cp.start()
            cp.wait()

        @pl.when(i > 0)
        def _():
            _ffn_step(k, x_ref, g_ref, wg_ref, wu_ref, wd_ref, o_ref, xn_ref)
    else:
        _ffn_step(k, x_ref, g_ref, wg_ref, wu_ref, wd_ref, o_ref, xn_ref)

    for src, dst in zip(side_in, side_out):
        dst[...] = src[...].astype(BF16)

    if final_norm:
        @pl.when(k == pl.num_programs(1) - 1)
        def _():
            o_ref[...] = _rms(o_ref[...], gf_ref[...])


def _ffn_head_kernel(x_ref, g_ref, wg_ref, wu_ref, wd_ref, o_ref, wgb_ref, wub_ref, wdb_ref, xn_ref):
    wgb_ref[...] = wg_ref[...].astype(BF16)
    wub_ref[...] = wu_ref[...].astype(BF16)
    wdb_ref[...] = wd_ref[...].astype(BF16)
    _ffn_step(pl.program_id(0), x_ref, g_ref, wgb_ref, wub_ref, wdb_ref, o_ref, xn_ref)


def _ffn_head(x, g, wg, wu, wd):
    tm, tf = TILES["ffn_head"]
    d = x.shape[1]
    dff = wg.shape[1]
    blocks = 2 * (tm * d * 4) * 2 + tm * d * 2 + 2 * 3 * d * tf * (4 + 2)
    temps = 3 * tm * tf * 4 + tm * d * 4
    col = lambda k: (0, k)
    row = lambda k: (k, 0)
    return pl.pallas_call(
        _ffn_head_kernel,
        grid=(dff // tf,),
        in_specs=[
            pl.BlockSpec((tm, d), lambda k: (0, 0)),
            _const_spec((1, d)),
            pl.BlockSpec((d, tf), col),
            pl.BlockSpec((d, tf), col),
            pl.BlockSpec((tf, d), row),
        ],
        out_specs=[
            pl.BlockSpec((tm, d), lambda k: (0, 0)),
            pl.BlockSpec((d, tf), col),
            pl.BlockSpec((d, tf), col),
            pl.BlockSpec((tf, d), row),
        ],
        out_shape=[
            jax.ShapeDtypeStruct((tm, d), F32),
            jax.ShapeDtypeStruct(wg.shape, BF16),
            jax.ShapeDtypeStruct(wu.shape, BF16),
            jax.ShapeDtypeStruct(wd.shape, BF16),
        ],
        scratch_shapes=[pltpu.VMEM((tm, d), BF16)],
        compiler_params=_params(blocks, temps, 1),
        name="ffn_head",
    )(x, g, wg, wu, wd)


def _cast_spec(shape, ni, nk):
    r, c = shape
    row_align = 2 * SUBLANES
    if c % (ni * LANES) == 0 and r % (nk * row_align) == 0:
        return pl.BlockSpec((r // nk, c // ni), lambda i, k: (k, i))
    m = max(m for m in range(1, nk + 1) if c % (m * LANES) == 0)
    assert r % (ni * row_align) == 0, shape
    return pl.BlockSpec((r // ni, c // m), lambda i, k: (i, jnp.minimum(k, m - 1)))


def _ffn(x, g, wg, wu, wd, gf, *, final_norm, side=(), head=None):
    tm, tf = TILES["ffn"]
    n, d = x.shape
    dff = wg.shape[1]
    ni, nk = n // tm, dff // tf
    adopt = head is not None
    if adopt:
        assert head.shape == (tm, d) and TILES["ffn_head"][0] == tm
        xi = lambda i: jnp.maximum(i, 1)
        wk = lambda i, k: jnp.where(i == 0, 0, k)
    else:
        xi = lambda i: i
        wk = lambda i, k: k
    side_specs = [_cast_spec(w.shape, ni, nk) for w in side]
    side_bytes = sum(2 * sp.block_shape[0] * sp.block_shape[1] * (4 + 2) for sp in side_specs)
    blocks = 2 * (tm * d * 4) * 2 + tm * d * 2 + 2 * 3 * d * tf * 2 + side_bytes
    temps = 3 * tm * tf * 4 + tm * d * 4
    outs = pl.pallas_call(
        functools.partial(_ffn_kernel, final_norm=final_norm, n_side=len(side), adopt=adopt),
        grid=(ni, nk),
        in_specs=[
            pl.BlockSpec((tm, d), lambda i, k: (xi(i), 0)),
            _const_spec((1, d)),
            pl.BlockSpec((d, tf), lambda i, k: (0, wk(i, k))),
            pl.BlockSpec((d, tf), lambda i, k: (0, wk(i, k))),
            pl.BlockSpec((tf, d), lambda i, k: (wk(i, k), 0)),
            _const_spec((1, d)),
        ] + ([pl.BlockSpec(memory_space=pl.ANY)] if adopt else []) + side_specs,
        out_specs=[pl.BlockSpec((tm, d), lambda i, k: (i, 0))] + side_specs,
        out_shape=[jax.ShapeDtypeStruct((n, d), F32)]
        + [jax.ShapeDtypeStruct(w.shape, BF16) for w in side],
        scratch_shapes=[pltpu.VMEM((tm, d), BF16)] + ([pltpu.SemaphoreType.DMA(())] if adopt else []),
        compiler_params=_params(blocks, temps, 2),
        name="ffn",
    )(x, g, wg, wu, wd, gf, *([head] if adopt else []), *side)
    return outs[0], list(outs[1:])


def _s5_prep_kernel(lr_ref, li_ref, ldt_ref, br_ref, bi_ref, cr_ref, ci_ref,
                    ar_ref, ai_ref, rb_ref, rc_ref, bbr_ref, bbi_ref, *, h):
    lr = lr_ref[...]
    li = li_ref[...]
    g, p = lr.shape
    dt = jnp.exp(ldt_ref[...])
    mag = jnp.exp(lr * dt)
    abar_re = mag * jnp.cos(li * dt)
    abar_im = mag * jnp.sin(li * dt)
    nr, ni = abar_re - 1.0, abar_im
    den = lr * lr + li * li
    fr = (nr * lr + ni * li) / den
    fi = (ni * lr - nr * li) / den
    ar_ref[...] = abar_re
    ai_ref[...] = abar_im
    for gg in range(g):
        rows = slice(gg * h, (gg + 1) * h)
        frg, fig = fr[gg:gg + 1, :], fi[gg:gg + 1, :]
        br, bi = br_ref[rows, :], bi_ref[rows, :]
        bbr_ref[rows, :] = frg * br - fig * bi
        bbi_ref[rows, :] = frg * bi + fig * br

    gl = LANES // h
    half = gl * p
    iota = lax.broadcasted_iota
    spread = (iota(jnp.int32, (p, half), 1) % p == iota(jnp.int32, (p, half), 0)).astype(BF16)
    spread_t = (iota(jnp.int32, (half, p), 0) % p == iota(jnp.int32, (half, p), 1)).astype(BF16)
    diag_b = iota(jnp.int32, (LANES, half), 0) // h == iota(jnp.int32, (LANES, half), 1) // p
    diag_c = iota(jnp.int32, (half, LANES), 0) // p == iota(jnp.int32, (half, LANES), 1) // h
    for j in range(g // gl):
        jj, j2 = divmod(j, 2)
        rows = slice(j * LANES, (j + 1) * LANES)
        for ri, (b_ref, c_ref, sign) in enumerate(((bbr_ref, cr_ref, 1.0), (bbi_ref, ci_ref, -1.0))):
            blk = jnp.dot(b_ref[rows, :].astype(BF16), spread, preferred_element_type=F32)
            rb_ref[jj, j2 * LANES:(j2 + 1) * LANES, ri * half:(ri + 1) * half] = (
                jnp.where(diag_b, blk, 0.0).astype(BF16))
            blk = lax.dot_general(spread_t, c_ref[rows, :].astype(BF16), (((1,), (1,)), ((), ())),
                                  preferred_element_type=F32)
            rc_ref[jj, ri * half:(ri + 1) * half, j2 * LANES:(j2 + 1) * LANES] = (
                jnp.where(diag_c, sign * blk, 0.0).astype(BF16))


def _s5_prep(a_re, a_im, log_dt, b_re, b_im, c_re, c_im):
    g, p, h = b_re.shape
    gl = LANES // h
    npair = g // (2 * gl)
    rows_gh = lambda a: a.reshape(g * h, p)
    return pl.pallas_call(
        functools.partial(_s5_prep_kernel, h=h),
        out_shape=(
            jax.ShapeDtypeStruct((g, p), F32),
            jax.ShapeDtypeStruct((g, p), F32),
            jax.ShapeDtypeStruct((npair, 2 * LANES, 2 * gl * p), BF16),
            jax.ShapeDtypeStruct((npair, 2 * gl * p, 2 * LANES), BF16),
        ),
        scratch_shapes=[pltpu.VMEM((g * h, p), F32), pltpu.VMEM((g * h, p), F32)],
        name="s5_prep",
    )(a_re, a_im, log_dt.reshape(g, 1),
      rows_gh(jnp.swapaxes(b_re, 1, 2)), rows_gh(jnp.swapaxes(b_im, 1, 2)), rows_gh(c_re), rows_gh(c_im))


def _mixer_kernel(h0_ref, hn_ref, gmix_ref, win_ref, rb_ref, rc_ref, at_ref, d_ref, wglu_ref, bglu_ref,
                  gs_ref, wp_ref, sc_ref, gp_ref, o_ref,
                  u_ref, ul_ref, s_ref, y_ref, yn_ref, st_ref, ext_ref, du_ref, xn_ref, *,
                  tc, nb, npair):
    c = pl.program_id(0)
    d = hn_ref.shape[-1]
    w = d // 2
    half = s_ref.shape[-1] // 2
    rows_per_t = 2 * nb
    nbuf = s_ref.shape[0]

    def in_proj(h_ref):
        h = h_ref[...].reshape(nb * tc, d)
        return jnp.dot(_rms(h, gmix_ref[...]).astype(BF16), win_ref[...], preferred_element_type=F32)

    @pl.when(c == 0)
    def _():
        ul_ref[...] = jnp.zeros(ul_ref.shape, F32)
        st_ref[...] = jnp.zeros(st_ref.shape, F32)
        ext_ref[:, :POOL_HALO, :] = jnp.zeros((nb, POOL_HALO, w), F32)
        u_ref[...] = in_proj(h0_ref)

    for jj in range(npair):
        for j2 in range(2):
            j = 2 * jj + j2
            for b in range(nb):
                ul_ref[jj, j2, pl.ds(b * 2 + j2, tc, stride=rows_per_t), :] = (
                    u_ref[b * tc:(b + 1) * tc, j * LANES:(j + 1) * LANES])
    du_ref[...] = d_ref[...] * u_ref[:, :w]
    for b in range(nb):
        ext_ref[b, POOL_HALO:, :] = u_ref[b * tc:(b + 1) * tc, w:]

    xn_ref[...] = _rms(hn_ref[...].reshape(nb * tc, d), gmix_ref[...]).astype(BF16)
    pw = d // npair
    gw = w // len(POOL_WINDOWS)
    t_idx = c * tc + lax.broadcasted_iota(jnp.int32, (tc, 1), 0)

    def next_u(p):
        u_ref[:, p * pw:(p + 1) * pw] = jnp.dot(
            xn_ref[...], win_ref[:, p * pw:(p + 1) * pw], preferred_element_type=F32)

    def b_proj(jj):
        lhs = jnp.concatenate([ul_ref[jj, 0], ul_ref[jj, 1]], axis=-1).astype(BF16)
        s_ref[jj % nbuf] = jnp.dot(lhs, rb_ref[jj], preferred_element_type=F32)

    def scan(jj):
        sb = jj % nbuf
        a_r = at_ref[jj, :, :half]
        a_i = at_ref[jj, :, half:]
        s_r = st_ref[jj, :, :half]
        s_i = st_ref[jj, :, half:]
        for t in range(tc):
            r0 = t * rows_per_t
            n_r = a_r * s_r - a_i * s_i + s_ref[sb, r0:r0 + rows_per_t, :half]
            n_i = a_r * s_i + a_i * s_r + s_ref[sb, r0:r0 + rows_per_t, half:]
            s_ref[sb, r0:r0 + rows_per_t, :half] = n_r
            s_ref[sb, r0:r0 + rows_per_t, half:] = n_i
            s_r, s_i = n_r, n_i
        st_ref[jj, :, :half] = s_r
        st_ref[jj, :, half:] = s_i

    def c_proj(jj):
        sb = jj % nbuf
        yy = jnp.dot(s_ref[sb].astype(BF16), rc_ref[jj], preferred_element_type=F32)
        y_ref[sb, 0] = yy[:, :LANES]
        y_ref[sb, 1] = yy[:, LANES:]
        for j2 in range(2):
            j = 2 * jj + j2
            for b in range(nb):
                yn_ref[b * tc:(b + 1) * tc, j * LANES:(j + 1) * LANES] = (
                    y_ref[sb, j2, pl.ds(b * 2 + j2, tc, stride=rows_per_t), :])

    def s5_out():
        y = yn_ref[...] + du_ref[...]
        y = jax.nn.gelu(y)
        z = jnp.dot(y.astype(BF16), wglu_ref[...], preferred_element_type=F32) + bglu_ref[...]
        y = y * jax.nn.sigmoid(z)
        o_ref[:, :, :w] = _rms(y, gs_ref[...]).astype(BF16).reshape(nb, tc, w)

    def pool(b):
        zs = []
        ssq = jnp.zeros((tc, 1), F32)
        for gi, win in enumerate(POOL_WINDOWS):
            e = ext_ref[b, :, gi * gw:(gi + 1) * gw]
            s = e
            shift = 1
            while shift < win:
                s = s + pltpu.roll(s, shift, axis=0)
                shift *= 2
            cnt = jnp.minimum(t_idx + 1, win).astype(F32)
            pooled = s[POOL_HALO:] / cnt - e[POOL_HALO:]
            z = jnp.dot(pooled.astype(BF16), wp_ref[gi], preferred_element_type=F32)
            z = z * sc_ref[:, gi * gw:(gi + 1) * gw]
            ssq = ssq + jnp.sum(z * z, axis=-1, keepdims=True)
            zs.append(z)
        inv = lax.rsqrt(ssq / w + EPS)
        for gi in range(len(POOL_WINDOWS)):
            o_ref[b, :, w + gi * gw:w + (gi + 1) * gw] = (
                zs[gi] * inv * gp_ref[:, gi * gw:(gi + 1) * gw]).astype(BF16)
        ext_ref[b, :POOL_HALO, :] = ext_ref[b, tc:, :]

    for jj in range(min(nbuf, npair)):
        b_proj(jj)
    for jj in range(npair):
        if jj < npair - 1:
            next_u(jj)
        scan(jj)
        for b in range(jj * nb // npair, (jj + 1) * nb // npair):
            pool(b)
        c_proj(jj)
        if jj + nbuf < npair:
            b_proj(jj + nbuf)
    next_u(npair - 1)
    s5_out()


def _mixer(h3, gmix, win, rb, rc, at, dskip, wglu, bglu, gs, wp, sc, gp):
    tc = TILES["mixer"]
    nb, L, d = h3.shape
    w = d // 2
    npair = rb.shape[0]
    nstate = rb.shape[2]
    rows = tc * 2 * nb
    nbuf = 2
    scratch = [
        pltpu.VMEM((nb * tc, d), F32),
        pltpu.VMEM((npair, 2, rows, LANES), F32),
        pltpu.VMEM((nbuf, rows, nstate), F32),
        pltpu.VMEM((nbuf, 2, rows, LANES), F32),
        pltpu.VMEM((nb * tc, w), F32),
        pltpu.VMEM((npair, 2 * nb, nstate), F32),
        pltpu.VMEM((nb, tc + POOL_HALO, w), F32),
        pltpu.VMEM((nb * tc, w), F32),
        pltpu.VMEM((nb * tc, d), BF16),
    ]
    blocks = (3 * nb * tc * d * 4 + 3 * nb * tc * d * 2 + d * d * 2 + rb.size * 2 + rc.size * 2
              + at.size * 4 + w * w * 2 + wp.size * 2
              + (nb * tc * d + npair * 2 * rows * LANES + nbuf * rows * nstate + nbuf * 2 * rows * LANES
                 + 2 * nb * tc * w + npair * 2 * nb * nstate + nb * (tc + POOL_HALO) * w) * 4)
    temps = nb * tc * d * 6 + rows * nstate * 6 + 4 * nb * tc * w * 4
    nchunks = L // tc
    return pl.pallas_call(
        functools.partial(_mixer_kernel, tc=tc, nb=nb, npair=npair),
        grid=(nchunks,),
        in_specs=[
            pl.BlockSpec((nb, tc, d), lambda c: (0, 0, 0), pipeline_mode=pl.Buffered(1)),
            pl.BlockSpec((nb, tc, d), lambda c: (0, jnp.minimum(c + 1, nchunks - 1), 0)),
            _const_spec((1, d)),
            _const_spec((d, d)),
            _const_spec(rb.shape),
            _const_spec(rc.shape),
            _const_spec(at.shape),
            _const_spec((1, w)),
            _const_spec((w, w)),
            _const_spec((1, w)),
            _const_spec((1, w)),
            _const_spec(wp.shape),
            _const_spec((1, w)),
            _const_spec((1, w)),
        ],
        out_specs=pl.BlockSpec((nb, tc, d), lambda c: (0, c, 0)),
        out_shape=jax.ShapeDtypeStruct((nb, L, d), BF16),
        scratch_shapes=scratch,
        compiler_params=_params(blocks, temps, 1),
        name="mixer",
    )(h3, h3, gmix, win, rb, rc, at, dskip, wglu, bglu, gs, wp, sc, gp)


def _kv_kernel(m_ref, g_ref, wk_ref, wv_ref, k_ref, v_ref, mn_ref):
    @pl.when(pl.program_id(0) == 0)
    def _():
        mn_ref[...] = _rms(m_ref[...], g_ref[...]).astype(BF16)

    mn = mn_ref[...]
    k_ref[...] = jnp.dot(mn, wk_ref[...], preferred_element_type=F32).astype(BF16)
    v_ref[...] = jnp.dot(mn, wv_ref[...], preferred_element_type=F32).astype(BF16)


def _kv(mem2, g, wk, wv):
    tn = TILES["kv"]
    n, d = mem2.shape
    blocks = n * d * 4 + n * d * 2 + 4 * d * tn * 2 + 4 * n * tn * 2
    temps = 2 * n * tn * 4 + n * d * 4
    return pl.pallas_call(
        _kv_kernel,
        grid=(d // tn,),
        in_specs=[
            _const_spec((n, d)),
            _const_spec((1, d)),
            pl.BlockSpec((d, tn), lambda j: (0, j)),
            pl.BlockSpec((d, tn), lambda j: (0, j)),
        ],
        out_specs=(pl.BlockSpec((n, tn), lambda j: (0, j)), pl.BlockSpec((n, tn), lambda j: (0, j))),
        out_shape=(jax.ShapeDtypeStruct((n, d), BF16), jax.ShapeDtypeStruct((n, d), BF16)),
        scratch_shapes=[pltpu.VMEM((n, d), BF16)],
        compiler_params=_params(blocks, temps, 1),
        name="mem_kv",
    )(mem2, g, wk, wv)


def _xattn_kernel(h_ref, m_ref, wout_ref, g_ref, wq_ref, k_ref, v_ref, wo_ref, o_ref):
    o_ref[...] = h_ref[...] + jnp.dot(m_ref[...], wout_ref[...], preferred_element_type=F32)
    h = o_ref[...]
    d = h.shape[-1]
    hd = d // MEM_HEADS
    q = jnp.dot(_rms(h, g_ref[...]).astype(BF16), wq_ref[...], preferred_element_type=F32)
    outs = []
    for hh in range(MEM_HEADS):
        qh = q[:, hh * hd:(hh + 1) * hd].astype(BF16)
        kh = k_ref[:, hh * hd:(hh + 1) * hd]
        s = lax.dot_general(qh, kh, (((1,), (1,)), ((), ())), preferred_element_type=F32)
        s = s * (hd ** -0.5)
        e = jnp.exp(s - jnp.max(s, axis=-1, keepdims=True))
        p = e / jnp.sum(e, axis=-1, keepdims=True)
        outs.append(jnp.dot(p.astype(BF16), v_ref[:, hh * hd:(hh + 1) * hd],
                            preferred_element_type=F32).astype(BF16))
    o = jnp.concatenate(outs, axis=-1)
    o_ref[...] = h + jnp.dot(o, wo_ref[...], preferred_element_type=F32)


def _xattn(h, m, wout, g, wq, k, v, wo, *, nb):
    tm = TILES["xattn"]
    n, d = h.shape
    nm = k.shape[0] // nb
    tiles = n // nb // tm
    rows = lambda b, i: (b * tiles + i, 0)
    blocks = 4 * tm * d * 4 + 2 * tm * d * 2 + 3 * d * d * 2 + 4 * nm * d * 2
    temps = 3 * tm * d * 4
    return pl.pallas_call(
        _xattn_kernel,
        grid=(nb, tiles),
        in_specs=[
            pl.BlockSpec((tm, d), rows),
            pl.BlockSpec((tm, d), rows),
            _const_spec((d, d)),
            _const_spec((1, d)),
            _const_spec((d, d)),
            pl.BlockSpec((nm, d), lambda b, i: (b, 0)),
            pl.BlockSpec((nm, d), lambda b, i: (b, 0)),
            _const_spec((d, d)),
        ],
        out_specs=pl.BlockSpec((tm, d), rows),
        out_shape=jax.ShapeDtypeStruct((n, d), F32),
        compiler_params=_params(blocks, temps, 2),
        name="xattn",
    )(h, m, wout, g, wq, k, v, wo)


def _s5_pole_table(abar_re, abar_im, npair, nb):
    g, p = abar_re.shape
    gl = g // (2 * npair)
    ab = jnp.stack([abar_re, abar_im]).reshape(2, npair, 2, gl, p)
    at = jnp.transpose(ab, (1, 2, 0, 3, 4)).reshape(npair, 1, 2, 2 * gl * p)
    return jnp.broadcast_to(at, (npair, nb, 2, 2 * gl * p)).reshape(npair, 2 * nb, 2 * gl * p)


def kernel(x, mem, g_ffn1, w1_gate, w1_up, w1_down, g_mix, w_in, ssm_a_re, ssm_a_im, ssm_log_dt,
           ssm_b_re, ssm_b_im, ssm_c_re, ssm_c_im, ssm_d, w_glu, b_glu, w_pool, pool_scale,
           g_out_ssm, g_out_pool, w_out, g_xattn, g_mem, w_q, w_k, w_v, w_o,
           g_ffn2, w2_gate, w2_up, w2_down, g_final):
    nb, L, d = x.shape
    n = nb * L
    depth = g_ffn1.shape[0]
    bf = lambda a: a.astype(BF16)
    row = lambda a: a.reshape(1, -1)

    h = x.reshape(n, d)
    for l in range(depth):
        last = l == depth - 1
        pool_shape = w_pool[l].shape
        later = [w2_gate[l], w2_up[l], w2_down[l], w_in[l], w_out[l], w_q[l], w_k[l], w_v[l], w_o[l],
                 w_glu[l], w_pool[l].reshape(-1, pool_shape[-1])]
        head, w1g, w1u, w1d = _ffn_head(h, row(g_ffn1[l]), w1_gate[l], w1_up[l], w1_down[l])
        h, later = _ffn(h, row(g_ffn1[l]), w1g, w1u, w1d, row(g_final),
                        final_norm=False, side=later, head=head)
        w2g, w2u, w2d, win, wout, wq, wk, wv, wo, wglu, wpool = later

        abar_re, abar_im, rb, rc = _s5_prep(ssm_a_re[l], ssm_a_im[l], ssm_log_dt[l],
                                            ssm_b_re[l], ssm_b_im[l], ssm_c_re[l], ssm_c_im[l])
        at = _s5_pole_table(abar_re, abar_im, rb.shape[0], nb)
        merged = _mixer(h.reshape(nb, L, d), row(g_mix[l]), win, rb, rc, at, row(ssm_d[l]),
                        wglu, row(b_glu[l]), row(g_out_ssm[l]),
                        wpool.reshape(pool_shape), row(pool_scale[l]), row(g_out_pool[l]))
        k, v = _kv(mem.reshape(nb * mem.shape[1], d), row(g_mem[l]), wk, wv)
        h = _xattn(h, merged.reshape(n, d), wout, row(g_xattn[l]), wq, k, v, wo, nb=nb)

        h, _ = _ffn(h, row(g_ffn2[l]), w2g, w2u, w2d, row(g_final), final_norm=last)
    return h.reshape(nb, L, d)
```

```python
import functools

import jax
import jax.numpy as jnp
from jax import lax
from jax.experimental import pallas as pl
from jax.experimental.pallas import tpu as pltpu

F32 = jnp.float32
BF16 = jnp.bfloat16

EPS = 1e-6
POOL_WINDOWS = (2, 4, 8, 16)
MEM_HEADS = 4

LANES = 128
SUBLANES = 8
MXU_COLS = 256
VMEM_BYTES_V7X = 64 * 1024 * 1024
VMEM_RESERVE = 2 * 1024 * 1024
POOL_HALO = 16

TILES = dict(ffn=(1024, 512), ffn_head=(1024, 256), mixer=128, xattn=512, kv=512)


def _params(block_bytes, temp_bytes, ndims):
    return pltpu.CompilerParams(
        dimension_semantics=("arbitrary",) * ndims,
        vmem_limit_bytes=int(min(VMEM_BYTES_V7X - VMEM_RESERVE, block_bytes + temp_bytes)),
    )


def _rms(x, g):
    return x * lax.rsqrt(jnp.mean(x * x, axis=-1, keepdims=True) + EPS) * g


def _const_spec(shape):
    nd = len(shape)
    return pl.BlockSpec(shape, lambda *_: (0,) * nd, pipeline_mode=pl.Buffered(1))


def _ffn_step(k, x_ref, g_ref, wg_ref, wu_ref, wd_ref, o_ref, xn_ref):
    def body(first):
        if first:
            xn_ref[...] = _rms(x_ref[...], g_ref[...]).astype(BF16)
        xn = xn_ref[...]
        npiece, _, hf = wg_ref.shape
        acts = []
        for p in range(npiece):
            gate = jnp.dot(xn, wg_ref[p], preferred_element_type=F32)
            up = jnp.dot(xn, wu_ref[p], preferred_element_type=F32)
            acts.append((jax.nn.silu(gate) * up * 0.5).astype(BF16))
        acc = None
        for p, act in enumerate(acts):
            part = jnp.dot(act, wd_ref[p * hf:(p + 1) * hf, :], preferred_element_type=F32)
            acc = part if acc is None else acc + part
        o_ref[...] = (x_ref[...] if first else o_ref[...]) + acc

    pl.when(k == 0)(functools.partial(body, True))
    pl.when(k > 0)(functools.partial(body, False))


def _ffn_kernel(*refs, final_norm, n_side, adopt):
    x_ref, g_ref, wg_ref, wu_ref, wd_ref, gf_ref = refs[:6]
    n_in = 6 + adopt
    side_in = refs[n_in:n_in + n_side]
    o_ref = refs[n_in + n_side]
    side_out = refs[n_in + n_side + 1:n_in + 2 * n_side + 1]
    xn_ref = refs[n_in + 2 * n_side + 1]
    i = pl.program_id(0)
    k = pl.program_id(1)

    if adopt:
        head_ref, sem = refs[6], refs[n_in + 2 * n_side + 2]

        @pl.when((i == 0) & (k == 0))
        def _():
            cp = pltpu.make_async_copy(head_ref, o_ref, sem)
            cp.start()
            cp.wait()

        @pl.when(i > 0)
        def _():
            _ffn_step(k, x_ref, g_ref, wg_ref, wu_ref, wd_ref, o_ref, xn_ref)
    else:
        _ffn_step(k, x_ref, g_ref, wg_ref, wu_ref, wd_ref, o_ref, xn_ref)

    for src, dst in zip(side_in, side_out):
        if len(dst.shape) == 3:
            for q in range(dst.shape[0]):
                dst[q] = src[:, q * MXU_COLS:(q + 1) * MXU_COLS].astype(BF16)
        else:
            dst[...] = src[...].astype(BF16)

    if final_norm:
        @pl.when(k == pl.num_programs(1) - 1)
        def _():
            o_ref[...] = _rms(o_ref[...], gf_ref[...])


def _ffn_head_kernel(x_ref, g_ref, wg_ref, wu_ref, wd_ref, o_ref, wgb_ref, wub_ref, wdb_ref, xn_ref):
    wgb_ref[0] = wg_ref[...].astype(BF16)
    wub_ref[0] = wu_ref[...].astype(BF16)
    wdb_ref[...] = wd_ref[...].astype(BF16)
    _ffn_step(pl.program_id(0), x_ref, g_ref, wgb_ref, wub_ref, wdb_ref, o_ref, xn_ref)


def _ffn_head(x, g, wg, wu, wd):
    tm, tf = TILES["ffn_head"]
    assert tf == MXU_COLS
    d = x.shape[1]
    dff = wg.shape[1]
    blocks = 2 * (tm * d * 4) * 2 + tm * d * 2 + 2 * 3 * d * tf * (4 + 2)
    temps = 3 * tm * tf * 4 + tm * d * 4
    col = lambda k: (0, k)
    row = lambda k: (k, 0)
    tile = lambda k: (k, 0, 0)
    return pl.pallas_call(
        _ffn_head_kernel,
        grid=(dff // tf,),
        in_specs=[
            pl.BlockSpec((tm, d), lambda k: (0, 0)),
            _const_spec((1, d)),
            pl.BlockSpec((d, tf), col),
            pl.BlockSpec((d, tf), col),
            pl.BlockSpec((tf, d), row),
        ],
        out_specs=[
            pl.BlockSpec((tm, d), lambda k: (0, 0)),
            pl.BlockSpec((1, d, tf), tile),
            pl.BlockSpec((1, d, tf), tile),
            pl.BlockSpec((tf, d), row),
        ],
        out_shape=[
            jax.ShapeDtypeStruct((tm, d), F32),
            jax.ShapeDtypeStruct((dff // tf, d, tf), BF16),
            jax.ShapeDtypeStruct((dff // tf, d, tf), BF16),
            jax.ShapeDtypeStruct(wd.shape, BF16),
        ],
        scratch_shapes=[pltpu.VMEM((tm, d), BF16)],
        compiler_params=_params(blocks, temps, 1),
        name="ffn_head",
    )(x, g, wg, wu, wd)


def _cast_spec(shape, ni, nk):
    r, c = shape
    row_align = 2 * SUBLANES
    if c % (ni * LANES) == 0 and r % (nk * row_align) == 0:
        return pl.BlockSpec((r // nk, c // ni), lambda i, k: (k, i))
    m = max(m for m in range(1, nk + 1) if c % (m * LANES) == 0)
    assert r % (ni * row_align) == 0, shape
    return pl.BlockSpec((r // ni, c // m), lambda i, k: (i, jnp.minimum(k, m - 1)))


def _ffn(x, g, wg, wu, wd, gf, *, final_norm, side=(), side_tiled=(), head=None):
    tm, tf = TILES["ffn"]
    n, d = x.shape
    dff = wd.shape[0]
    ni, nk = n // tm, dff // tf
    pieces = tf // MXU_COLS
    adopt = head is not None
    if adopt:
        assert head.shape == (tm, d) and TILES["ffn_head"][0] == tm
        xi = lambda i: jnp.maximum(i, 1)
        wk = lambda i, k: jnp.where(i == 0, 0, k)
    else:
        xi = lambda i: i
        wk = lambda i, k: k
    side_specs = [_cast_spec(w.shape, ni, nk) for w in side]
    side_out_specs = list(side_specs)
    side_shapes = [jax.ShapeDtypeStruct(w.shape, BF16) for w in side]
    for w in side_tiled:
        r, c = w.shape
        assert c == dff and r % (ni * 2 * SUBLANES) == 0
        side_specs.append(pl.BlockSpec((r // ni, tf), lambda i, k: (i, k)))
        side_out_specs.append(pl.BlockSpec((pieces, r // ni, MXU_COLS), lambda i, k: (k, i, 0)))
        side_shapes.append(jax.ShapeDtypeStruct((c // MXU_COLS, r, MXU_COLS), BF16))
    n_side = len(side_specs)
    side_bytes = sum(2 * sp.block_shape[0] * sp.block_shape[1] * (4 + 2) for sp in side_specs)
    blocks = 2 * (tm * d * 4) * 2 + tm * d * 2 + 2 * 3 * d * tf * 2 + side_bytes
    temps = 3 * tm * tf * 4 + tm * d * 4
    outs = pl.pallas_call(
        functools.partial(_ffn_kernel, final_norm=final_norm, n_side=n_side, adopt=adopt),
        grid=(ni, nk),
        in_specs=[
            pl.BlockSpec((tm, d), lambda i, k: (xi(i), 0)),
            _const_spec((1, d)),
            pl.BlockSpec((pieces, d, MXU_COLS), lambda i, k: (wk(i, k), 0, 0)),
            pl.BlockSpec((pieces, d, MXU_COLS), lambda i, k: (wk(i, k), 0, 0)),
            pl.BlockSpec((tf, d), lambda i, k: (wk(i, k), 0)),
            _const_spec((1, d)),
        ] + ([pl.BlockSpec(memory_space=pl.ANY)] if adopt else []) + side_specs,
        out_specs=[pl.BlockSpec((tm, d), lambda i, k: (i, 0))] + side_out_specs,
        out_shape=[jax.ShapeDtypeStruct((n, d), F32)] + side_shapes,
        scratch_shapes=[pltpu.VMEM((tm, d), BF16)] + ([pltpu.SemaphoreType.DMA(())] if adopt else []),
        compiler_params=_params(blocks, temps, 2),
        name="ffn",
    )(x, g, wg, wu, wd, gf, *([head] if adopt else []), *side, *side_tiled)
    return outs[0], list(outs[1:])


def _s5_prep_kernel(lr_ref, li_ref, ldt_ref, br_ref, bi_ref, cr_ref, ci_ref,
                    ar_ref, ai_ref, rb_ref, rc_ref, bbr_ref, bbi_ref, *, h):
    lr = lr_ref[...]
    li = li_ref[...]
    g, p = lr.shape
    dt = jnp.exp(ldt_ref[...])
    mag = jnp.exp(lr * dt)
    abar_re = mag * jnp.cos(li * dt)
    abar_im = mag * jnp.sin(li * dt)
    nr, ni = abar_re - 1.0, abar_im
    den = lr * lr + li * li
    fr = (nr * lr + ni * li) / den
    fi = (ni * lr - nr * li) / den
    ar_ref[...] = abar_re
    ai_ref[...] = abar_im
    for gg in range(g):
        rows = slice(gg * h, (gg + 1) * h)
        frg, fig = fr[gg:gg + 1, :], fi[gg:gg + 1, :]
        br, bi = br_ref[rows, :], bi_ref[rows, :]
        bbr_ref[rows, :] = frg * br - fig * bi
        bbi_ref[rows, :] = frg * bi + fig * br

    gl = LANES // h
    half = gl * p
    iota = lax.broadcasted_iota
    spread = (iota(jnp.int32, (p, half), 1) % p == iota(jnp.int32, (p, half), 0)).astype(BF16)
    spread_t = (iota(jnp.int32, (half, p), 0) % p == iota(jnp.int32, (half, p), 1)).astype(BF16)
    diag_b = iota(jnp.int32, (LANES, half), 0) // h == iota(jnp.int32, (LANES, half), 1) // p
    diag_c = iota(jnp.int32, (half, LANES), 0) // p == iota(jnp.int32, (half, LANES), 1) // h
    for j in range(g // gl):
        jj, j2 = divmod(j, 2)
        rows = slice(j * LANES, (j + 1) * LANES)
        for ri, (b_ref, c_ref, sign) in enumerate(((bbr_ref, cr_ref, 1.0), (bbi_ref, ci_ref, -1.0))):
            blk = jnp.dot(b_ref[rows, :].astype(BF16), spread, preferred_element_type=F32)
            rb_ref[jj, j2 * LANES:(j2 + 1) * LANES, ri * half:(ri + 1) * half] = (
                jnp.where(diag_b, blk, 0.0).astype(BF16))
            blk = lax.dot_general(spread_t, c_ref[rows, :].astype(BF16), (((1,), (1,)), ((), ())),
                                  preferred_element_type=F32)
            rc_ref[jj, ri * half:(ri + 1) * half, j2 * LANES:(j2 + 1) * LANES] = (
                jnp.where(diag_c, sign * blk, 0.0).astype(BF16))


def _s5_prep(a_re, a_im, log_dt, b_re, b_im, c_re, c_im):
    g, p, h = b_re.shape
    gl = LANES // h
    npair = g // (2 * gl)
    rows_gh = lambda a: a.reshape(g * h, p)
    return pl.pallas_call(
        functools.partial(_s5_prep_kernel, h=h),
        out_shape=(
            jax.ShapeDtypeStruct((g, p), F32),
            jax.ShapeDtypeStruct((g, p), F32),
            jax.ShapeDtypeStruct((npair, 2 * LANES, 2 * gl * p), BF16),
            jax.ShapeDtypeStruct((npair, 2 * gl * p, 2 * LANES), BF16),
        ),
        scratch_shapes=[pltpu.VMEM((g * h, p), F32), pltpu.VMEM((g * h, p), F32)],
        name="s5_prep",
    )(a_re, a_im, log_dt.reshape(g, 1),
      rows_gh(jnp.swapaxes(b_re, 1, 2)), rows_gh(jnp.swapaxes(b_im, 1, 2)), rows_gh(c_re), rows_gh(c_im))


def _mixer_kernel(h0_ref, hn_ref, gmix_ref, win_ref, rb_ref, rc_ref, at_ref, d_ref, wglu_ref, bglu_ref,
                  gs_ref, wp_ref, sc_ref, gp_ref, o_ref,
                  u_ref, ul_ref, s_ref, y_ref, yn_ref, st_ref, ext_ref, du_ref, xn_ref, *,
                  tc, nb, npair):
    c = pl.program_id(0)
    d = hn_ref.shape[-1]
    w = d // 2
    half = s_ref.shape[-1] // 2
    rows_per_t = 2 * nb
    nbuf = s_ref.shape[0]

    def in_proj(h_ref):
        h = h_ref[...].reshape(nb * tc, d)
        return jnp.dot(_rms(h, gmix_ref[...]).astype(BF16), win_ref[...], preferred_element_type=F32)

    @pl.when(c == 0)
    def _():
        ul_ref[...] = jnp.zeros(ul_ref.shape, F32)
        st_ref[...] = jnp.zeros(st_ref.shape, F32)
        ext_ref[:, :POOL_HALO, :] = jnp.zeros((nb, POOL_HALO, w), F32)
        u_ref[...] = in_proj(h0_ref)

    for jj in range(npair):
        for j2 in range(2):
            j = 2 * jj + j2
            for b in range(nb):
                ul_ref[jj, j2, pl.ds(b * 2 + j2, tc, stride=rows_per_t), :] = (
                    u_ref[b * tc:(b + 1) * tc, j * LANES:(j + 1) * LANES])
    du_ref[...] = d_ref[...] * u_ref[:, :w]
    for b in range(nb):
        ext_ref[b, POOL_HALO:, :] = u_ref[b * tc:(b + 1) * tc, w:]

    xn_ref[...] = _rms(hn_ref[...].reshape(nb * tc, d), gmix_ref[...]).astype(BF16)
    pw = d // npair
    gw = w // len(POOL_WINDOWS)
    t_idx = c * tc + lax.broadcasted_iota(jnp.int32, (tc, 1), 0)

    def next_u(p):
        u_ref[:, p * pw:(p + 1) * pw] = jnp.dot(
            xn_ref[...], win_ref[:, p * pw:(p + 1) * pw], preferred_element_type=F32)

    def b_proj(jj):
        lhs = jnp.concatenate([ul_ref[jj, 0], ul_ref[jj, 1]], axis=-1).astype(BF16)
        s_ref[jj % nbuf] = jnp.dot(lhs, rb_ref[jj], preferred_element_type=F32)

    def scan(jj):
        sb = jj % nbuf
        a_r = at_ref[jj, :, :half]
        a_i = at_ref[jj, :, half:]
        s_r = st_ref[jj, :, :half]
        s_i = st_ref[jj, :, half:]
        for t in range(tc):
            r0 = t * rows_per_t
            n_r = a_r * s_r - a_i * s_i + s_ref[sb, r0:r0 + rows_per_t, :half]
            n_i = a_r * s_i + a_i * s_r + s_ref[sb, r0:r0 + rows_per_t, half:]
            s_ref[sb, r0:r0 + rows_per_t, :half] = n_r
            s_ref[sb, r0:r0 + rows_per_t, half:] = n_i
            s_r, s_i = n_r, n_i
        st_ref[jj, :, :half] = s_r
        st_ref[jj, :, half:] = s_i

    def c_proj(jj):
        sb = jj % nbuf
        yy = jnp.dot(s_ref[sb].astype(BF16), rc_ref[jj], preferred_element_type=F32)
        y_ref[sb, 0] = yy[:, :LANES]
        y_ref[sb, 1] = yy[:, LANES:]
        for j2 in range(2):
            j = 2 * jj + j2
            for b in range(nb):
                yn_ref[b * tc:(b + 1) * tc, j * LANES:(j + 1) * LANES] = (
                    y_ref[sb, j2, pl.ds(b * 2 + j2, tc, stride=rows_per_t), :])

    def s5_out():
        y = yn_ref[...] + du_ref[...]
        y = jax.nn.gelu(y)
        z = jnp.dot(y.astype(BF16), wglu_ref[...], preferred_element_type=F32) + bglu_ref[...]
        y = y * jax.nn.sigmoid(z)
        o_ref[:, :, :w] = _rms(y, gs_ref[...]).astype(BF16).reshape(nb, tc, w)

    def pool(b):
        zs = []
        ssq = jnp.zeros((tc, 1), F32)
        for gi, win in enumerate(POOL_WINDOWS):
            e = ext_ref[b, :, gi * gw:(gi + 1) * gw]
            s = e
            shift = 1
            while shift < win:
                s = s + pltpu.roll(s, shift, axis=0)
                shift *= 2
            cnt = jnp.minimum(t_idx + 1, win).astype(F32)
            pooled = s[POOL_HALO:] / cnt - e[POOL_HALO:]
            z = jnp.dot(pooled.astype(BF16), wp_ref[gi], preferred_element_type=F32)
            z = z * sc_ref[:, gi * gw:(gi + 1) * gw]
            ssq = ssq + jnp.sum(z * z, axis=-1, keepdims=True)
            zs.append(z)
        inv = lax.rsqrt(ssq / w + EPS)
        for gi in range(len(POOL_WINDOWS)):
            o_ref[b, :, w + gi * gw:w + (gi + 1) * gw] = (
                zs[gi] * inv * gp_ref[:, gi * gw:(gi + 1) * gw]).astype(BF16)
        ext_ref[b, :POOL_HALO, :] = ext_ref[b, tc:, :]

    for jj in range(min(nbuf, npair)):
        b_proj(jj)
    for jj in range(npair):
        if jj < npair - 1:
            next_u(jj)
        scan(jj)
        for b in range(jj * nb // npair, (jj + 1) * nb // npair):
            pool(b)
        c_proj(jj)
        if jj + nbuf < npair:
            b_proj(jj + nbuf)
    next_u(npair - 1)
    s5_out()


def _mixer(h3, gmix, win, rb, rc, at, dskip, wglu, bglu, gs, wp, sc, gp):
    tc = TILES["mixer"]
    nb, L, d = h3.shape
    w = d // 2
    npair = rb.shape[0]
    nstate = rb.shape[2]
    rows = tc * 2 * nb
    nbuf = 2
    scratch = [
        pltpu.VMEM((nb * tc, d), F32),
        pltpu.VMEM((npair, 2, rows, LANES), F32),
        pltpu.VMEM((nbuf, rows, nstate), F32),
        pltpu.VMEM((nbuf, 2, rows, LANES), F32),
        pltpu.VMEM((nb * tc, w), F32),
        pltpu.VMEM((npair, 2 * nb, nstate), F32),
        pltpu.VMEM((nb, tc + POOL_HALO, w), F32),
        pltpu.VMEM((nb * tc, w), F32),
        pltpu.VMEM((nb * tc, d), BF16),
    ]
    blocks = (3 * nb * tc * d * 4 + 3 * nb * tc * d * 2 + d * d * 2 + rb.size * 2 + rc.size * 2
              + at.size * 4 + w * w * 2 + wp.size * 2
              + (nb * tc * d + npair * 2 * rows * LANES + nbuf * rows * nstate + nbuf * 2 * rows * LANES
                 + 2 * nb * tc * w + npair * 2 * nb * nstate + nb * (tc + POOL_HALO) * w) * 4)
    temps = nb * tc * d * 6 + rows * nstate * 6 + 4 * nb * tc * w * 4
    nchunks = L // tc
    return pl.pallas_call(
        functools.partial(_mixer_kernel, tc=tc, nb=nb, npair=npair),
        grid=(nchunks,),
        in_specs=[
            pl.BlockSpec((nb, tc, d), lambda c: (0, 0, 0), pipeline_mode=pl.Buffered(1)),
            pl.BlockSpec((nb, tc, d), lambda c: (0, jnp.minimum(c + 1, nchunks - 1), 0)),
            _const_spec((1, d)),
            _const_spec((d, d)),
            _const_spec(rb.shape),
            _const_spec(rc.shape),
            _const_spec(at.shape),
            _const_spec((1, w)),
            _const_spec((w, w)),
            _const_spec((1, w)),
            _const_spec((1, w)),
            _const_spec(wp.shape),
            _const_spec((1, w)),
            _const_spec((1, w)),
        ],
        out_specs=pl.BlockSpec((nb, tc, d), lambda c: (0, c, 0)),
        out_shape=jax.ShapeDtypeStruct((nb, L, d), BF16),
        scratch_shapes=scratch,
        compiler_params=_params(blocks, temps, 1),
        name="mixer",
    )(h3, h3, gmix, win, rb, rc, at, dskip, wglu, bglu, gs, wp, sc, gp)


def _kv_kernel(m_ref, g_ref, wk_ref, wv_ref, k_ref, v_ref, mn_ref):
    @pl.when(pl.program_id(0) == 0)
    def _():
        mn_ref[...] = _rms(m_ref[...], g_ref[...]).astype(BF16)

    mn = mn_ref[...]
    k_ref[...] = jnp.dot(mn, wk_ref[...], preferred_element_type=F32).astype(BF16)
    v_ref[...] = jnp.dot(mn, wv_ref[...], preferred_element_type=F32).astype(BF16)


def _kv(mem2, g, wk, wv):
    tn = TILES["kv"]
    n, d = mem2.shape
    blocks = n * d * 4 + n * d * 2 + 4 * d * tn * 2 + 4 * n * tn * 2
    temps = 2 * n * tn * 4 + n * d * 4
    return pl.pallas_call(
        _kv_kernel,
        grid=(d // tn,),
        in_specs=[
            _const_spec((n, d)),
            _const_spec((1, d)),
            pl.BlockSpec((d, tn), lambda j: (0, j)),
            pl.BlockSpec((d, tn), lambda j: (0, j)),
        ],
        out_specs=(pl.BlockSpec((n, tn), lambda j: (0, j)), pl.BlockSpec((n, tn), lambda j: (0, j))),
        out_shape=(jax.ShapeDtypeStruct((n, d), BF16), jax.ShapeDtypeStruct((n, d), BF16)),
        scratch_shapes=[pltpu.VMEM((n, d), BF16)],
        compiler_params=_params(blocks, temps, 1),
        name="mem_kv",
    )(mem2, g, wk, wv)


def _xattn_kernel(h_ref, m_ref, wout_ref, g_ref, wq_ref, k_ref, v_ref, wo_ref, o_ref):
    o_ref[...] = h_ref[...] + jnp.dot(m_ref[...], wout_ref[...], preferred_element_type=F32)
    h = o_ref[...]
    d = h.shape[-1]
    hd = d // MEM_HEADS
    q = jnp.dot(_rms(h, g_ref[...]).astype(BF16), wq_ref[...], preferred_element_type=F32)
    outs = []
    for hh in range(MEM_HEADS):
        qh = q[:, hh * hd:(hh + 1) * hd].astype(BF16)
        kh = k_ref[:, hh * hd:(hh + 1) * hd]
        s = lax.dot_general(qh, kh, (((1,), (1,)), ((), ())), preferred_element_type=F32)
        s = s * (hd ** -0.5)
        e = jnp.exp(s - jnp.max(s, axis=-1, keepdims=True))
        p = e / jnp.sum(e, axis=-1, keepdims=True)
        outs.append(jnp.dot(p.astype(BF16), v_ref[:, hh * hd:(hh + 1) * hd],
                            preferred_element_type=F32).astype(BF16))
    o = jnp.concatenate(outs, axis=-1)
    o_ref[...] = h + jnp.dot(o, wo_ref[...], preferred_element_type=F32)


def _xattn(h, m, wout, g, wq, k, v, wo, *, nb):
    tm = TILES["xattn"]
    n, d = h.shape
    nm = k.shape[0] // nb
    tiles = n // nb // tm
    rows = lambda b, i: (b * tiles + i, 0)
    blocks = 4 * tm * d * 4 + 2 * tm * d * 2 + 3 * d * d * 2 + 4 * nm * d * 2
    temps = 3 * tm * d * 4
    return pl.pallas_call(
        _xattn_kernel,
        grid=(nb, tiles),
        in_specs=[
            pl.BlockSpec((tm, d), rows),
            pl.BlockSpec((tm, d), rows),
            _const_spec((d, d)),
            _const_spec((1, d)),
            _const_spec((d, d)),
            pl.BlockSpec((nm, d), lambda b, i: (b, 0)),
            pl.BlockSpec((nm, d), lambda b, i: (b, 0)),
            _const_spec((d, d)),
        ],
        out_specs=pl.BlockSpec((tm, d), rows),
        out_shape=jax.ShapeDtypeStruct((n, d), F32),
        compiler_params=_params(blocks, temps, 2),
        name="xattn",
    )(h, m, wout, g, wq, k, v, wo)


def _s5_pole_table(abar_re, abar_im, npair, nb):
    g, p = abar_re.shape
    gl = g // (2 * npair)
    ab = jnp.stack([abar_re, abar_im]).reshape(2, npair, 2, gl, p)
    at = jnp.transpose(ab, (1, 2, 0, 3, 4)).reshape(npair, 1, 2, 2 * gl * p)
    return jnp.broadcast_to(at, (npair, nb, 2, 2 * gl * p)).reshape(npair, 2 * nb, 2 * gl * p)


def kernel(x, mem, g_ffn1, w1_gate, w1_up, w1_down, g_mix, w_in, ssm_a_re, ssm_a_im, ssm_log_dt,
           ssm_b_re, ssm_b_im, ssm_c_re, ssm_c_im, ssm_d, w_glu, b_glu, w_pool, pool_scale,
           g_out_ssm, g_out_pool, w_out, g_xattn, g_mem, w_q, w_k, w_v, w_o,
           g_ffn2, w2_gate, w2_up, w2_down, g_final):
    nb, L, d = x.shape
    n = nb * L
    depth = g_ffn1.shape[0]
    bf = lambda a: a.astype(BF16)
    row = lambda a: a.reshape(1, -1)

    h = x.reshape(n, d)
    for l in range(depth):
        last = l == depth - 1
        pool_shape = w_pool[l].shape
        later = [w2_down[l], w_in[l], w_out[l], w_q[l], w_k[l], w_v[l], w_o[l],
                 w_glu[l], w_pool[l].reshape(-1, pool_shape[-1])]
        head, w1g, w1u, w1d = _ffn_head(h, row(g_ffn1[l]), w1_gate[l], w1_up[l], w1_down[l])
        h, later = _ffn(h, row(g_ffn1[l]), w1g, w1u, w1d, row(g_final), final_norm=False,
                        side=later, side_tiled=[w2_gate[l], w2_up[l]], head=head)
        w2d, win, wout, wq, wk, wv, wo, wglu, wpool, w2g, w2u = later

        abar_re, abar_im, rb, rc = _s5_prep(ssm_a_re[l], ssm_a_im[l], ssm_log_dt[l],
                                            ssm_b_re[l], ssm_b_im[l], ssm_c_re[l], ssm_c_im[l])
        at = _s5_pole_table(abar_re, abar_im, rb.shape[0], nb)
        merged = _mixer(h.reshape(nb, L, d), row(g_mix[l]), win, rb, rc, at, row(ssm_d[l]),
                        wglu, row(b_glu[l]), row(g_out_ssm[l]),
                        wpool.reshape(pool_shape), row(pool_scale[l]), row(g_out_pool[l]))
        k, v = _kv(mem.reshape(nb * mem.shape[1], d), row(g_mem[l]), wk, wv)
        h = _xattn(h, merged.reshape(n, d), wout, row(g_xattn[l]), wq, k, v, wo, nb=nb)

        h, _ = _ffn(h, row(g_ffn2[l]), w2g, w2u, w2d, row(g_final), final_norm=last)
    return h.reshape(nb, L, d)
```

```python
import functools

import jax
import jax.numpy as jnp
from jax import lax
from jax.experimental import pallas as pl
from jax.experimental.pallas import tpu as pltpu

F32 = jnp.float32
BF16 = jnp.bfloat16

EPS = 1e-6
POOL_WINDOWS = (2, 4, 8, 16)
MEM_HEADS = 4

LANES = 128
SUBLANES = 8
MXU_COLS = 256
VMEM_BYTES_V7X = 64 * 1024 * 1024
VMEM_RESERVE = 2 * 1024 * 1024
POOL_HALO = 16

TILES = dict(ffn=(1024, 512), ffn_head=(1024, 256), mixer=128, xattn=512, kv=1024)


def _params(block_bytes, temp_bytes, ndims):
    return pltpu.CompilerParams(
        dimension_semantics=("arbitrary",) * ndims,
        vmem_limit_bytes=int(min(VMEM_BYTES_V7X - VMEM_RESERVE, block_bytes + temp_bytes)),
    )


def _rms(x, g):
    return x * lax.rsqrt(jnp.mean(x * x, axis=-1, keepdims=True) + EPS) * g


def _const_spec(shape):
    nd = len(shape)
    return pl.BlockSpec(shape, lambda *_: (0,) * nd, pipeline_mode=pl.Buffered(1))


def _ffn_step(k, nk, x_ref, g_ref, wg_ref, wu_ref, wd_ref, o_ref, xn_ref, gf_ref=None):
    def body(first, last=False):
        if first:
            xn_ref[...] = _rms(x_ref[...], g_ref[...]).astype(BF16)
        xn = xn_ref[...]
        npiece, _, hf = wg_ref.shape
        acts = []
        for p in range(npiece):
            gate = jnp.dot(xn, wg_ref[p], preferred_element_type=F32)
            up = jnp.dot(xn, wu_ref[p], preferred_element_type=F32)
            acts.append((jax.nn.silu(gate) * up * 0.5).astype(BF16))
        acc = None
        for p, act in enumerate(acts):
            part = jnp.dot(act, wd_ref[p * hf:(p + 1) * hf, :], preferred_element_type=F32)
            acc = part if acc is None else acc + part
        o = (x_ref[...] if first else o_ref[...]) + acc
        o_ref[...] = _rms(o, gf_ref[...]) if last else o

    pl.when(k == 0)(functools.partial(body, True))
    if gf_ref is None:
        pl.when(k > 0)(functools.partial(body, False))
    else:
        pl.when((k > 0) & (k < nk - 1))(functools.partial(body, False))
        pl.when(k == nk - 1)(functools.partial(body, False, True))


def _ffn_kernel(*refs, final_norm, n_side, adopt):
    x_ref, g_ref, wg_ref, wu_ref, wd_ref, gf_ref = refs[:6]
    n_in = 6 + adopt
    side_in = refs[n_in:n_in + n_side]
    o_ref = refs[n_in + n_side]
    side_out = refs[n_in + n_side + 1:n_in + 2 * n_side + 1]
    xn_ref = refs[n_in + 2 * n_side + 1]
    i = pl.program_id(0)
    k = pl.program_id(1)
    nk = pl.num_programs(1)
    gf = gf_ref if final_norm else None

    if adopt:
        head_ref, sem = refs[6], refs[n_in + 2 * n_side + 2]

        @pl.when((i == 0) & (k == 0))
        def _():
            cp = pltpu.make_async_copy(head_ref, o_ref, sem)
            cp.start()
            cp.wait()

        @pl.when(i > 0)
        def _():
            _ffn_step(k, nk, x_ref, g_ref, wg_ref, wu_ref, wd_ref, o_ref, xn_ref, gf)
    else:
        _ffn_step(k, nk, x_ref, g_ref, wg_ref, wu_ref, wd_ref, o_ref, xn_ref, gf)

    for src, dst in zip(side_in, side_out):
        if len(dst.shape) == 3:
            for q in range(dst.shape[0]):
                dst[q] = src[:, q * MXU_COLS:(q + 1) * MXU_COLS].astype(BF16)
        else:
            dst[...] = src[...].astype(BF16)


def _ffn_head_kernel(x_ref, g_ref, wg_ref, wu_ref, wd_ref, o_ref, wgb_ref, wub_ref, wdb_ref, xn_ref):
    wgb_ref[0] = wg_ref[...].astype(BF16)
    wub_ref[0] = wu_ref[...].astype(BF16)
    wdb_ref[...] = wd_ref[...].astype(BF16)
    _ffn_step(pl.program_id(0), pl.num_programs(0), x_ref, g_ref, wgb_ref, wub_ref, wdb_ref, o_ref, xn_ref)


def _ffn_head(x, g, wg, wu, wd):
    tm, tf = TILES["ffn_head"]
    assert tf == MXU_COLS
    d = x.shape[1]
    dff = wg.shape[1]
    blocks = 2 * (tm * d * 4) * 2 + tm * d * 2 + 2 * 3 * d * tf * (4 + 2)
    temps = 3 * tm * tf * 4 + tm * d * 4
    col = lambda k: (0, k)
    row = lambda k: (k, 0)
    tile = lambda k: (k, 0, 0)
    return pl.pallas_call(
        _ffn_head_kernel,
        grid=(dff // tf,),
        in_specs=[
            pl.BlockSpec((tm, d), lambda k: (0, 0)),
            _const_spec((1, d)),
            pl.BlockSpec((d, tf), col),
            pl.BlockSpec((d, tf), col),
            pl.BlockSpec((tf, d), row),
        ],
        out_specs=[
            pl.BlockSpec((tm, d), lambda k: (0, 0)),
            pl.BlockSpec((1, d, tf), tile),
            pl.BlockSpec((1, d, tf), tile),
            pl.BlockSpec((tf, d), row),
        ],
        out_shape=[
            jax.ShapeDtypeStruct((tm, d), F32),
            jax.ShapeDtypeStruct((dff // tf, d, tf), BF16),
            jax.ShapeDtypeStruct((dff // tf, d, tf), BF16),
            jax.ShapeDtypeStruct(wd.shape, BF16),
        ],
        scratch_shapes=[pltpu.VMEM((tm, d), BF16)],
        compiler_params=_params(blocks, temps, 1),
        name="ffn_head",
    )(x, g, wg, wu, wd)


def _cast_spec(shape, ni, nk):
    r, c = shape
    row_align = 2 * SUBLANES
    if c % (ni * LANES) == 0 and r % (nk * row_align) == 0:
        return pl.BlockSpec((r // nk, c // ni), lambda i, k: (k, i))
    m = max(m for m in range(1, nk + 1) if c % (m * LANES) == 0)
    assert r % (ni * row_align) == 0, shape
    return pl.BlockSpec((r // ni, c // m), lambda i, k: (i, jnp.minimum(k, m - 1)))


def _ffn(x, g, wg, wu, wd, gf, *, final_norm, side=(), side_tiled=(), head=None):
    tm, tf = TILES["ffn"]
    n, d = x.shape
    dff = wd.shape[0]
    ni, nk = n // tm, dff // tf
    pieces = tf // MXU_COLS
    adopt = head is not None
    if adopt:
        assert head.shape == (tm, d) and TILES["ffn_head"][0] == tm
        xi = lambda i: jnp.maximum(i, 1)
        wk = lambda i, k: jnp.where(i == 0, 0, k)
    else:
        xi = lambda i: i
        wk = lambda i, k: k
    side_specs = [_cast_spec(w.shape, ni, nk) for w in side]
    side_out_specs = list(side_specs)
    side_shapes = [jax.ShapeDtypeStruct(w.shape, BF16) for w in side]
    for w in side_tiled:
        r, c = w.shape
        assert c == dff and r % (ni * 2 * SUBLANES) == 0
        side_specs.append(pl.BlockSpec((r // ni, tf), lambda i, k: (i, k)))
        side_out_specs.append(pl.BlockSpec((pieces, r // ni, MXU_COLS), lambda i, k: (k, i, 0)))
        side_shapes.append(jax.ShapeDtypeStruct((c // MXU_COLS, r, MXU_COLS), BF16))
    n_side = len(side_specs)
    side_bytes = sum(2 * sp.block_shape[0] * sp.block_shape[1] * (4 + 2) for sp in side_specs)
    blocks = 2 * (tm * d * 4) * 2 + tm * d * 2 + 2 * 3 * d * tf * 2 + side_bytes
    temps = 3 * tm * tf * 4 + tm * d * 4
    outs = pl.pallas_call(
        functools.partial(_ffn_kernel, final_norm=final_norm, n_side=n_side, adopt=adopt),
        grid=(ni, nk),
        in_specs=[
            pl.BlockSpec((tm, d), lambda i, k: (xi(i), 0)),
            _const_spec((1, d)),
            pl.BlockSpec((pieces, d, MXU_COLS), lambda i, k: (wk(i, k), 0, 0)),
            pl.BlockSpec((pieces, d, MXU_COLS), lambda i, k: (wk(i, k), 0, 0)),
            pl.BlockSpec((tf, d), lambda i, k: (wk(i, k), 0)),
            _const_spec((1, d)),
        ] + ([pl.BlockSpec(memory_space=pl.ANY)] if adopt else []) + side_specs,
        out_specs=[pl.BlockSpec((tm, d), lambda i, k: (i, 0))] + side_out_specs,
        out_shape=[jax.ShapeDtypeStruct((n, d), F32)] + side_shapes,
        scratch_shapes=[pltpu.VMEM((tm, d), BF16)] + ([pltpu.SemaphoreType.DMA(())] if adopt else []),
        compiler_params=_params(blocks, temps, 2),
        name="ffn",
    )(x, g, wg, wu, wd, gf, *([head] if adopt else []), *side, *side_tiled)
    return outs[0], list(outs[1:])


def _s5_prep_kernel(lr_ref, li_ref, ldt_ref, br_ref, bi_ref, cr_ref, ci_ref,
                    ar_ref, ai_ref, rb_ref, rc_ref, bbr_ref, bbi_ref, *, h):
    lr = lr_ref[...]
    li = li_ref[...]
    g, p = lr.shape
    dt = jnp.exp(ldt_ref[...])
    mag = jnp.exp(lr * dt)
    abar_re = mag * jnp.cos(li * dt)
    abar_im = mag * jnp.sin(li * dt)
    nr, ni = abar_re - 1.0, abar_im
    den = lr * lr + li * li
    fr = (nr * lr + ni * li) / den
    fi = (ni * lr - nr * li) / den
    ar_ref[...] = abar_re
    ai_ref[...] = abar_im
    for gg in range(g):
        rows = slice(gg * h, (gg + 1) * h)
        frg, fig = fr[gg:gg + 1, :], fi[gg:gg + 1, :]
        br, bi = br_ref[rows, :], bi_ref[rows, :]
        bbr_ref[rows, :] = frg * br - fig * bi
        bbi_ref[rows, :] = frg * bi + fig * br

    gl = LANES // h
    half = gl * p
    iota = lax.broadcasted_iota
    spread = (iota(jnp.int32, (p, half), 1) % p == iota(jnp.int32, (p, half), 0)).astype(BF16)
    spread_t = (iota(jnp.int32, (half, p), 0) % p == iota(jnp.int32, (half, p), 1)).astype(BF16)
    diag_b = iota(jnp.int32, (LANES, half), 0) // h == iota(jnp.int32, (LANES, half), 1) // p
    diag_c = iota(jnp.int32, (half, LANES), 0) // p == iota(jnp.int32, (half, LANES), 1) // h
    for j in range(g // gl):
        jj, j2 = divmod(j, 2)
        rows = slice(j * LANES, (j + 1) * LANES)
        for ri, (b_ref, c_ref, sign) in enumerate(((bbr_ref, cr_ref, 1.0), (bbi_ref, ci_ref, -1.0))):
            blk = jnp.dot(b_ref[rows, :].astype(BF16), spread, preferred_element_type=F32)
            rb_ref[jj, j2 * LANES:(j2 + 1) * LANES, ri * half:(ri + 1) * half] = (
                jnp.where(diag_b, blk, 0.0).astype(BF16))
            blk = lax.dot_general(spread_t, c_ref[rows, :].astype(BF16), (((1,), (1,)), ((), ())),
                                  preferred_element_type=F32)
            rc_ref[jj, ri * half:(ri + 1) * half, j2 * LANES:(j2 + 1) * LANES] = (
                jnp.where(diag_c, sign * blk, 0.0).astype(BF16))


def _s5_prep(a_re, a_im, log_dt, b_re, b_im, c_re, c_im):
    g, p, h = b_re.shape
    gl = LANES // h
    npair = g // (2 * gl)
    rows_gh = lambda a: a.reshape(g * h, p)
    return pl.pallas_call(
        functools.partial(_s5_prep_kernel, h=h),
        out_shape=(
            jax.ShapeDtypeStruct((g, p), F32),
            jax.ShapeDtypeStruct((g, p), F32),
            jax.ShapeDtypeStruct((npair, 2 * LANES, 2 * gl * p), BF16),
            jax.ShapeDtypeStruct((npair, 2 * gl * p, 2 * LANES), BF16),
        ),
        scratch_shapes=[pltpu.VMEM((g * h, p), F32), pltpu.VMEM((g * h, p), F32)],
        name="s5_prep",
    )(a_re, a_im, log_dt.reshape(g, 1),
      rows_gh(jnp.swapaxes(b_re, 1, 2)), rows_gh(jnp.swapaxes(b_im, 1, 2)), rows_gh(c_re), rows_gh(c_im))


def _mixer_kernel(h0_ref, hn_ref, gmix_ref, win_ref, rb_ref, rc_ref, at_ref, d_ref, wglu_ref, bglu_ref,
                  gs_ref, wp_ref, sc_ref, gp_ref, o_ref,
                  u_ref, ul_ref, s_ref, y_ref, yn_ref, st_ref, ext_ref, du_ref, xn_ref, *,
                  tc, nb, npair):
    c = pl.program_id(0)
    d = hn_ref.shape[-1]
    w = d // 2
    half = s_ref.shape[-1] // 2
    rows_per_t = 2 * nb
    nbuf = s_ref.shape[0]

    def in_proj(h_ref):
        h = h_ref[...].reshape(nb * tc, d)
        return jnp.dot(_rms(h, gmix_ref[...]).astype(BF16), win_ref[...], preferred_element_type=F32)

    @pl.when(c == 0)
    def _():
        ul_ref[...] = jnp.zeros(ul_ref.shape, F32)
        st_ref[...] = jnp.zeros(st_ref.shape, F32)
        ext_ref[:, :POOL_HALO, :] = jnp.zeros((nb, POOL_HALO, w), F32)
        u_ref[...] = in_proj(h0_ref)

    for jj in range(npair):
        for j2 in range(2):
            j = 2 * jj + j2
            for b in range(nb):
                ul_ref[jj, j2, pl.ds(b * 2 + j2, tc, stride=rows_per_t), :] = (
                    u_ref[b * tc:(b + 1) * tc, j * LANES:(j + 1) * LANES])
    du_ref[...] = d_ref[...] * u_ref[:, :w]
    for b in range(nb):
        ext_ref[b, POOL_HALO:, :] = u_ref[b * tc:(b + 1) * tc, w:]

    xn_ref[...] = _rms(hn_ref[...].reshape(nb * tc, d), gmix_ref[...]).astype(BF16)
    pw = d // npair
    gw = w // len(POOL_WINDOWS)
    t_idx = c * tc + lax.broadcasted_iota(jnp.int32, (tc, 1), 0)

    def next_u(p):
        u_ref[:, p * pw:(p + 1) * pw] = jnp.dot(
            xn_ref[...], win_ref[:, p * pw:(p + 1) * pw], preferred_element_type=F32)

    def b_proj(jj):
        lhs = jnp.concatenate([ul_ref[jj, 0], ul_ref[jj, 1]], axis=-1).astype(BF16)
        s_ref[jj % nbuf] = jnp.dot(lhs, rb_ref[jj], preferred_element_type=F32)

    def scan(jj):
        sb = jj % nbuf
        a_r = at_ref[jj, :, :half]
        a_i = at_ref[jj, :, half:]
        s_r = st_ref[jj, :, :half]
        s_i = st_ref[jj, :, half:]
        for t in range(tc):
            r0 = t * rows_per_t
            n_r = a_r * s_r - a_i * s_i + s_ref[sb, r0:r0 + rows_per_t, :half]
            n_i = a_r * s_i + a_i * s_r + s_ref[sb, r0:r0 + rows_per_t, half:]
            s_ref[sb, r0:r0 + rows_per_t, :half] = n_r
            s_ref[sb, r0:r0 + rows_per_t, half:] = n_i
            s_r, s_i = n_r, n_i
        st_ref[jj, :, :half] = s_r
        st_ref[jj, :, half:] = s_i

    def c_proj(jj):
        sb = jj % nbuf
        yy = jnp.dot(s_ref[sb].astype(BF16), rc_ref[jj], preferred_element_type=F32)
        y_ref[sb, 0] = yy[:, :LANES]
        y_ref[sb, 1] = yy[:, LANES:]
        for j2 in range(2):
            j = 2 * jj + j2
            for b in range(nb):
                yn_ref[b * tc:(b + 1) * tc, j * LANES:(j + 1) * LANES] = (
                    y_ref[sb, j2, pl.ds(b * 2 + j2, tc, stride=rows_per_t), :])

    def s5_out():
        y = yn_ref[...] + du_ref[...]
        y = jax.nn.gelu(y)
        z = jnp.dot(y.astype(BF16), wglu_ref[...], preferred_element_type=F32) + bglu_ref[...]
        y = y * jax.nn.sigmoid(z)
        o_ref[:, :, :w] = _rms(y, gs_ref[...]).astype(BF16).reshape(nb, tc, w)

    def pool(b):
        zs = []
        ssq = jnp.zeros((tc, 1), F32)
        for gi, win in enumerate(POOL_WINDOWS):
            e = ext_ref[b, :, gi * gw:(gi + 1) * gw]
            s = e
            shift = 1
            while shift < win:
                s = s + pltpu.roll(s, shift, axis=0)
                shift *= 2
            cnt = jnp.minimum(t_idx + 1, win).astype(F32)
            pooled = s[POOL_HALO:] / cnt - e[POOL_HALO:]
            z = jnp.dot(pooled.astype(BF16), wp_ref[gi], preferred_element_type=F32)
            z = z * sc_ref[:, gi * gw:(gi + 1) * gw]
            ssq = ssq + jnp.sum(z * z, axis=-1, keepdims=True)
            zs.append(z)
        inv = lax.rsqrt(ssq / w + EPS)
        for gi in range(len(POOL_WINDOWS)):
            o_ref[b, :, w + gi * gw:w + (gi + 1) * gw] = (
                zs[gi] * inv * gp_ref[:, gi * gw:(gi + 1) * gw]).astype(BF16)
        ext_ref[b, :POOL_HALO, :] = ext_ref[b, tc:, :]

    for jj in range(min(nbuf, npair)):
        b_proj(jj)
    for jj in range(npair):
        if jj < npair - 1:
            next_u(jj)
        scan(jj)
        for b in range(jj * nb // npair, (jj + 1) * nb // npair):
            pool(b)
        c_proj(jj)
        if jj + nbuf < npair:
            b_proj(jj + nbuf)
    next_u(npair - 1)
    s5_out()


def _mixer(h3, gmix, win, rb, rc, at, dskip, wglu, bglu, gs, wp, sc, gp):
    tc = TILES["mixer"]
    nb, L, d = h3.shape
    w = d // 2
    npair = rb.shape[0]
    nstate = rb.shape[2]
    rows = tc * 2 * nb
    nbuf = 2
    scratch = [
        pltpu.VMEM((nb * tc, d), F32),
        pltpu.VMEM((npair, 2, rows, LANES), F32),
        pltpu.VMEM((nbuf, rows, nstate), F32),
        pltpu.VMEM((nbuf, 2, rows, LANES), F32),
        pltpu.VMEM((nb * tc, w), F32),
        pltpu.VMEM((npair, 2 * nb, nstate), F32),
        pltpu.VMEM((nb, tc + POOL_HALO, w), F32),
        pltpu.VMEM((nb * tc, w), F32),
        pltpu.VMEM((nb * tc, d), BF16),
    ]
    blocks = (3 * nb * tc * d * 4 + 3 * nb * tc * d * 2 + d * d * 2 + rb.size * 2 + rc.size * 2
              + at.size * 4 + w * w * 2 + wp.size * 2
              + (nb * tc * d + npair * 2 * rows * LANES + nbuf * rows * nstate + nbuf * 2 * rows * LANES
                 + 2 * nb * tc * w + npair * 2 * nb * nstate + nb * (tc + POOL_HALO) * w) * 4)
    temps = nb * tc * d * 6 + rows * nstate * 6 + 4 * nb * tc * w * 4
    nchunks = L // tc
    return pl.pallas_call(
        functools.partial(_mixer_kernel, tc=tc, nb=nb, npair=npair),
        grid=(nchunks,),
        in_specs=[
            pl.BlockSpec((nb, tc, d), lambda c: (0, 0, 0), pipeline_mode=pl.Buffered(1)),
            pl.BlockSpec((nb, tc, d), lambda c: (0, jnp.minimum(c + 1, nchunks - 1), 0)),
            _const_spec((1, d)),
            _const_spec((d, d)),
            _const_spec(rb.shape),
            _const_spec(rc.shape),
            _const_spec(at.shape),
            _const_spec((1, w)),
            _const_spec((w, w)),
            _const_spec((1, w)),
            _const_spec((1, w)),
            _const_spec(wp.shape),
            _const_spec((1, w)),
            _const_spec((1, w)),
        ],
        out_specs=pl.BlockSpec((nb, tc, d), lambda c: (0, c, 0)),
        out_shape=jax.ShapeDtypeStruct((nb, L, d), BF16),
        scratch_shapes=scratch,
        compiler_params=_params(blocks, temps, 1),
        name="mixer",
    )(h3, h3, gmix, win, rb, rc, at, dskip, wglu, bglu, gs, wp, sc, gp)


def _kv_kernel(m_ref, g_ref, wk_ref, wv_ref, k_ref, v_ref, mn_ref):
    @pl.when(pl.program_id(0) == 0)
    def _():
        mn_ref[...] = _rms(m_ref[...], g_ref[...]).astype(BF16)

    mn = mn_ref[...]
    k_ref[...] = jnp.dot(mn, wk_ref[...], preferred_element_type=F32).astype(BF16)
    v_ref[...] = jnp.dot(mn, wv_ref[...], preferred_element_type=F32).astype(BF16)


def _kv(mem2, g, wk, wv):
    tn = TILES["kv"]
    n, d = mem2.shape
    blocks = n * d * 4 + n * d * 2 + 4 * d * tn * 2 + 4 * n * tn * 2
    temps = 2 * n * tn * 4 + n * d * 4
    return pl.pallas_call(
        _kv_kernel,
        grid=(d // tn,),
        in_specs=[
            _const_spec((n, d)),
            _const_spec((1, d)),
            pl.BlockSpec((d, tn), lambda j: (0, j)),
            pl.BlockSpec((d, tn), lambda j: (0, j)),
        ],
        out_specs=(pl.BlockSpec((n, tn), lambda j: (0, j)), pl.BlockSpec((n, tn), lambda j: (0, j))),
        out_shape=(jax.ShapeDtypeStruct((n, d), BF16), jax.ShapeDtypeStruct((n, d), BF16)),
        scratch_shapes=[pltpu.VMEM((n, d), BF16)],
        compiler_params=_params(blocks, temps, 1),
        name="mem_kv",
    )(mem2, g, wk, wv)


def _xattn_kernel(h_ref, m_ref, wout_ref, g_ref, wq_ref, k_ref, v_ref, wo_ref, o_ref):
    o_ref[...] = h_ref[...] + jnp.dot(m_ref[...], wout_ref[...], preferred_element_type=F32)
    h = o_ref[...]
    d = h.shape[-1]
    hd = d // MEM_HEADS
    q = jnp.dot(_rms(h, g_ref[...]).astype(BF16), wq_ref[...], preferred_element_type=F32)
    outs = []
    for hh in range(MEM_HEADS):
        qh = q[:, hh * hd:(hh + 1) * hd].astype(BF16)
        kh = k_ref[:, hh * hd:(hh + 1) * hd]
        s = lax.dot_general(qh, kh, (((1,), (1,)), ((), ())), preferred_element_type=F32)
        s = s * (hd ** -0.5)
        e = jnp.exp(s - jnp.max(s, axis=-1, keepdims=True))
        p = e / jnp.sum(e, axis=-1, keepdims=True)
        outs.append(jnp.dot(p.astype(BF16), v_ref[:, hh * hd:(hh + 1) * hd],
                            preferred_element_type=F32).astype(BF16))
    o = jnp.concatenate(outs, axis=-1)
    o_ref[...] = h + jnp.dot(o, wo_ref[...], preferred_element_type=F32)


def _xattn(h, m, wout, g, wq, k, v, wo, *, nb):
    tm = TILES["xattn"]
    n, d = h.shape
    nm = k.shape[0] // nb
    tiles = n // nb // tm
    rows = lambda b, i: (b * tiles + i, 0)
    blocks = 4 * tm * d * 4 + 2 * tm * d * 2 + 3 * d * d * 2 + 4 * nm * d * 2
    temps = 3 * tm * d * 4
    return pl.pallas_call(
        _xattn_kernel,
        grid=(nb, tiles),
        in_specs=[
            pl.BlockSpec((tm, d), rows),
            pl.BlockSpec((tm, d), rows),
            _const_spec((d, d)),
            _const_spec((1, d)),
            _const_spec((d, d)),
            pl.BlockSpec((nm, d), lambda b, i: (b, 0)),
            pl.BlockSpec((nm, d), lambda b, i: (b, 0)),
            _const_spec((d, d)),
        ],
        out_specs=pl.BlockSpec((tm, d), rows),
        out_shape=jax.ShapeDtypeStruct((n, d), F32),
        compiler_params=_params(blocks, temps, 2),
        name="xattn",
    )(h, m, wout, g, wq, k, v, wo)


def _s5_pole_table(abar_re, abar_im, npair, nb):
    g, p = abar_re.shape
    gl = g // (2 * npair)
    ab = jnp.stack([abar_re, abar_im]).reshape(2, npair, 2, gl, p)
    at = jnp.transpose(ab, (1, 2, 0, 3, 4)).reshape(npair, 1, 2, 2 * gl * p)
    return jnp.broadcast_to(at, (npair, nb, 2, 2 * gl * p)).reshape(npair, 2 * nb, 2 * gl * p)


def kernel(x, mem, g_ffn1, w1_gate, w1_up, w1_down, g_mix, w_in, ssm_a_re, ssm_a_im, ssm_log_dt,
           ssm_b_re, ssm_b_im, ssm_c_re, ssm_c_im, ssm_d, w_glu, b_glu, w_pool, pool_scale,
           g_out_ssm, g_out_pool, w_out, g_xattn, g_mem, w_q, w_k, w_v, w_o,
           g_ffn2, w2_gate, w2_up, w2_down, g_final):
    nb, L, d = x.shape
    n = nb * L
    depth = g_ffn1.shape[0]
    bf = lambda a: a.astype(BF16)
    row = lambda a: a.reshape(1, -1)

    h = x.reshape(n, d)
    for l in range(depth):
        last = l == depth - 1
        pool_shape = w_pool[l].shape
        later = [w2_down[l], w_in[l], w_out[l], w_q[l], w_k[l], w_v[l], w_o[l],
                 w_glu[l], w_pool[l].reshape(-1, pool_shape[-1])]
        head, w1g, w1u, w1d = _ffn_head(h, row(g_ffn1[l]), w1_gate[l], w1_up[l], w1_down[l])
        h, later = _ffn(h, row(g_ffn1[l]), w1g, w1u, w1d, row(g_final), final_norm=False,
                        side=later, side_tiled=[w2_gate[l], w2_up[l]], head=head)
        w2d, win, wout, wq, wk, wv, wo, wglu, wpool, w2g, w2u = later

        abar_re, abar_im, rb, rc = _s5_prep(ssm_a_re[l], ssm_a_im[l], ssm_log_dt[l],
                                            ssm_b_re[l], ssm_b_im[l], ssm_c_re[l], ssm_c_im[l])
        at = _s5_pole_table(abar_re, abar_im, rb.shape[0], nb)
        merged = _mixer(h.reshape(nb, L, d), row(g_mix[l]), win, rb, rc, at, row(ssm_d[l]),
                        wglu, row(b_glu[l]), row(g_out_ssm[l]),
                        wpool.reshape(pool_shape), row(pool_scale[l]), row(g_out_pool[l]))
        k, v = _kv(mem.reshape(nb * mem.shape[1], d), row(g_mem[l]), wk, wv)
        h = _xattn(h, merged.reshape(n, d), wout, row(g_xattn[l]), wq, k, v, wo, nb=nb)

        h, _ = _ffn(h, row(g_ffn2[l]), w2g, w2u, w2d, row(g_final), final_norm=last)
    return h.reshape(nb, L, d)
```

```python
import functools

import jax
import jax.numpy as jnp
from jax import lax
from jax.experimental import pallas as pl
from jax.experimental.pallas import tpu as pltpu

F32 = jnp.float32
BF16 = jnp.bfloat16

EPS = 1e-6
POOL_WINDOWS = (2, 4, 8, 16)
MEM_HEADS = 4

LANES = 128
SUBLANES = 8
MXU_COLS = 256
VMEM_BYTES_V7X = 64 * 1024 * 1024
VMEM_RESERVE = 2 * 1024 * 1024
POOL_HALO = 16

TILES = dict(ffn=(1024, 512), ffn_head=(1024, 256), mixer=128, xattn=512, kv=512)


def _params(block_bytes, temp_bytes, ndims):
    return pltpu.CompilerParams(
        dimension_semantics=("arbitrary",) * ndims,
        vmem_limit_bytes=int(min(VMEM_BYTES_V7X - VMEM_RESERVE, block_bytes + temp_bytes)),
    )


def _rms(x, g):
    return x * lax.rsqrt(jnp.mean(x * x, axis=-1, keepdims=True) + EPS) * g


def _const_spec(shape):
    nd = len(shape)
    return pl.BlockSpec(shape, lambda *_: (0,) * nd, pipeline_mode=pl.Buffered(1))


def _ffn_step(k, nk, x_ref, g_ref, wg_ref, wu_ref, wd_ref, o_ref, xn_ref, gf_ref=None):
    def body(first, last=False):
        if first:
            xn_ref[...] = _rms(x_ref[...], g_ref[...]).astype(BF16)
        xn = xn_ref[...]
        npiece, _, hf = wg_ref.shape
        acts = []
        for p in range(npiece):
            gate = jnp.dot(xn, wg_ref[p], preferred_element_type=F32)
            up = jnp.dot(xn, wu_ref[p], preferred_element_type=F32)
            acts.append((jax.nn.silu(gate) * up * 0.5).astype(BF16))
        acc = None
        for p, act in enumerate(acts):
            part = jnp.dot(act, wd_ref[p * hf:(p + 1) * hf, :], preferred_element_type=F32)
            acc = part if acc is None else acc + part
        o = (x_ref[...] if first else o_ref[...]) + acc
        o_ref[...] = _rms(o, gf_ref[...]) if last else o

    pl.when(k == 0)(functools.partial(body, True))
    if gf_ref is None:
        pl.when(k > 0)(functools.partial(body, False))
    else:
        pl.when((k > 0) & (k < nk - 1))(functools.partial(body, False))
        pl.when(k == nk - 1)(functools.partial(body, False, True))


def _ffn_kernel(*refs, final_norm, n_side, adopt):
    x_ref, g_ref, wg_ref, wu_ref, wd_ref, gf_ref = refs[:6]
    n_in = 6 + adopt
    side_in = refs[n_in:n_in + n_side]
    o_ref = refs[n_in + n_side]
    side_out = refs[n_in + n_side + 1:n_in + 2 * n_side + 1]
    xn_ref = refs[n_in + 2 * n_side + 1]
    i = pl.program_id(0)
    k = pl.program_id(1)
    nk = pl.num_programs(1)
    gf = gf_ref if final_norm else None

    if adopt:
        head_ref, sem = refs[6], refs[n_in + 2 * n_side + 2]

        @pl.when((i == 0) & (k == 0))
        def _():
            cp = pltpu.make_async_copy(head_ref, o_ref, sem)
            cp.start()
            cp.wait()

        @pl.when(i > 0)
        def _():
            _ffn_step(k, nk, x_ref, g_ref, wg_ref, wu_ref, wd_ref, o_ref, xn_ref, gf)
    else:
        _ffn_step(k, nk, x_ref, g_ref, wg_ref, wu_ref, wd_ref, o_ref, xn_ref, gf)

    for src, dst in zip(side_in, side_out):
        if len(dst.shape) == 3:
            for q in range(dst.shape[0]):
                dst[q] = src[:, q * MXU_COLS:(q + 1) * MXU_COLS].astype(BF16)
        else:
            dst[...] = src[...].astype(BF16)


def _ffn_head_kernel(x_ref, g_ref, wg_ref, wu_ref, wd_ref, o_ref, wgb_ref, wub_ref, wdb_ref, xn_ref):
    wgb_ref[0] = wg_ref[...].astype(BF16)
    wub_ref[0] = wu_ref[...].astype(BF16)
    wdb_ref[...] = wd_ref[...].astype(BF16)
    _ffn_step(pl.program_id(0), pl.num_programs(0), x_ref, g_ref, wgb_ref, wub_ref, wdb_ref, o_ref, xn_ref)


def _ffn_head(x, g, wg, wu, wd):
    tm, tf = TILES["ffn_head"]
    assert tf == MXU_COLS
    d = x.shape[1]
    dff = wg.shape[1]
    blocks = 2 * (tm * d * 4) * 2 + tm * d * 2 + 2 * 3 * d * tf * (4 + 2)
    temps = 3 * tm * tf * 4 + tm * d * 4
    col = lambda k: (0, k)
    row = lambda k: (k, 0)
    tile = lambda k: (k, 0, 0)
    return pl.pallas_call(
        _ffn_head_kernel,
        grid=(dff // tf,),
        in_specs=[
            pl.BlockSpec((tm, d), lambda k: (0, 0)),
            _const_spec((1, d)),
            pl.BlockSpec((d, tf), col),
            pl.BlockSpec((d, tf), col),
            pl.BlockSpec((tf, d), row),
        ],
        out_specs=[
            pl.BlockSpec((tm, d), lambda k: (0, 0)),
            pl.BlockSpec((1, d, tf), tile),
            pl.BlockSpec((1, d, tf), tile),
            pl.BlockSpec((tf, d), row),
        ],
        out_shape=[
            jax.ShapeDtypeStruct((tm, d), F32),
            jax.ShapeDtypeStruct((dff // tf, d, tf), BF16),
            jax.ShapeDtypeStruct((dff // tf, d, tf), BF16),
            jax.ShapeDtypeStruct(wd.shape, BF16),
        ],
        scratch_shapes=[pltpu.VMEM((tm, d), BF16)],
        compiler_params=_params(blocks, temps, 1),
        name="ffn_head",
    )(x, g, wg, wu, wd)


def _cast_spec(shape, ni, nk):
    r, c = shape
    row_align = 2 * SUBLANES
    if c % (ni * LANES) == 0 and r % (nk * row_align) == 0:
        return pl.BlockSpec((r // nk, c // ni), lambda i, k: (k, i))
    m = max(m for m in range(1, nk + 1) if c % (m * LANES) == 0)
    assert r % (ni * row_align) == 0, shape
    return pl.BlockSpec((r // ni, c // m), lambda i, k: (i, jnp.minimum(k, m - 1)))


def _ffn(x, g, wg, wu, wd, gf, *, final_norm, side=(), side_tiled=(), head=None):
    tm, tf = TILES["ffn"]
    n, d = x.shape
    dff = wd.shape[0]
    ni, nk = n // tm, dff // tf
    pieces = tf // MXU_COLS
    adopt = head is not None
    if adopt:
        assert head.shape == (tm, d) and TILES["ffn_head"][0] == tm
        xi = lambda i: jnp.maximum(i, 1)
        wk = lambda i, k: jnp.where(i == 0, 0, k)
    else:
        xi = lambda i: i
        wk = lambda i, k: k
    side_specs = [_cast_spec(w.shape, ni, nk) for w in side]
    side_out_specs = list(side_specs)
    side_shapes = [jax.ShapeDtypeStruct(w.shape, BF16) for w in side]
    for w in side_tiled:
        r, c = w.shape
        assert c == dff and r % (ni * 2 * SUBLANES) == 0
        side_specs.append(pl.BlockSpec((r // ni, tf), lambda i, k: (i, k)))
        side_out_specs.append(pl.BlockSpec((pieces, r // ni, MXU_COLS), lambda i, k: (k, i, 0)))
        side_shapes.append(jax.ShapeDtypeStruct((c // MXU_COLS, r, MXU_COLS), BF16))
    n_side = len(side_specs)
    side_bytes = sum(2 * sp.block_shape[0] * sp.block_shape[1] * (4 + 2) for sp in side_specs)
    blocks = 2 * (tm * d * 4) * 2 + tm * d * 2 + 2 * 3 * d * tf * 2 + side_bytes
    temps = 3 * tm * tf * 4 + tm * d * 4
    outs = pl.pallas_call(
        functools.partial(_ffn_kernel, final_norm=final_norm, n_side=n_side, adopt=adopt),
        grid=(ni, nk),
        in_specs=[
            pl.BlockSpec((tm, d), lambda i, k: (xi(i), 0)),
            _const_spec((1, d)),
            pl.BlockSpec((pieces, d, MXU_COLS), lambda i, k: (wk(i, k), 0, 0)),
            pl.BlockSpec((pieces, d, MXU_COLS), lambda i, k: (wk(i, k), 0, 0)),
            pl.BlockSpec((tf, d), lambda i, k: (wk(i, k), 0)),
            _const_spec((1, d)),
        ] + ([pl.BlockSpec(memory_space=pl.ANY)] if adopt else []) + side_specs,
        out_specs=[pl.BlockSpec((tm, d), lambda i, k: (i, 0))] + side_out_specs,
        out_shape=[jax.ShapeDtypeStruct((n, d), F32)] + side_shapes,
        scratch_shapes=[pltpu.VMEM((tm, d), BF16)] + ([pltpu.SemaphoreType.DMA(())] if adopt else []),
        compiler_params=_params(blocks, temps, 2),
        name="ffn",
    )(x, g, wg, wu, wd, gf, *([head] if adopt else []), *side, *side_tiled)
    return outs[0], list(outs[1:])


def _s5_prep_kernel(lr_ref, li_ref, ldt_ref, br_ref, bi_ref, cr_ref, ci_ref,
                    ar_ref, ai_ref, rb_ref, rc_ref, bbr_ref, bbi_ref, *, h):
    lr = lr_ref[...]
    li = li_ref[...]
    g, p = lr.shape
    dt = jnp.exp(ldt_ref[...])
    mag = jnp.exp(lr * dt)
    abar_re = mag * jnp.cos(li * dt)
    abar_im = mag * jnp.sin(li * dt)
    nr, ni = abar_re - 1.0, abar_im
    den = lr * lr + li * li
    fr = (nr * lr + ni * li) / den
    fi = (ni * lr - nr * li) / den
    ar_ref[...] = abar_re
    ai_ref[...] = abar_im
    for gg in range(g):
        rows = slice(gg * h, (gg + 1) * h)
        frg, fig = fr[gg:gg + 1, :], fi[gg:gg + 1, :]
        br, bi = br_ref[rows, :], bi_ref[rows, :]
        bbr_ref[rows, :] = frg * br - fig * bi
        bbi_ref[rows, :] = frg * bi + fig * br

    gl = LANES // h
    half = gl * p
    iota = lax.broadcasted_iota
    spread = (iota(jnp.int32, (p, half), 1) % p == iota(jnp.int32, (p, half), 0)).astype(BF16)
    spread_t = (iota(jnp.int32, (half, p), 0) % p == iota(jnp.int32, (half, p), 1)).astype(BF16)
    diag_b = iota(jnp.int32, (LANES, half), 0) // h == iota(jnp.int32, (LANES, half), 1) // p
    diag_c = iota(jnp.int32, (half, LANES), 0) // p == iota(jnp.int32, (half, LANES), 1) // h
    for j in range(g // gl):
        jj, j2 = divmod(j, 2)
        rows = slice(j * LANES, (j + 1) * LANES)
        for ri, (b_ref, c_ref, sign) in enumerate(((bbr_ref, cr_ref, 1.0), (bbi_ref, ci_ref, -1.0))):
            blk = jnp.dot(b_ref[rows, :].astype(BF16), spread, preferred_element_type=F32)
            rb_ref[jj, j2 * LANES:(j2 + 1) * LANES, ri * half:(ri + 1) * half] = (
                jnp.where(diag_b, blk, 0.0).astype(BF16))
            blk = lax.dot_general(spread_t, c_ref[rows, :].astype(BF16), (((1,), (1,)), ((), ())),
                                  preferred_element_type=F32)
            rc_ref[jj, ri * half:(ri + 1) * half, j2 * LANES:(j2 + 1) * LANES] = (
                jnp.where(diag_c, sign * blk, 0.0).astype(BF16))


def _s5_prep(a_re, a_im, log_dt, b_re, b_im, c_re, c_im):
    g, p, h = b_re.shape
    gl = LANES // h
    npair = g // (2 * gl)
    rows_gh = lambda a: a.reshape(g * h, p)
    return pl.pallas_call(
        functools.partial(_s5_prep_kernel, h=h),
        out_shape=(
            jax.ShapeDtypeStruct((g, p), F32),
            jax.ShapeDtypeStruct((g, p), F32),
            jax.ShapeDtypeStruct((npair, 2 * LANES, 2 * gl * p), BF16),
            jax.ShapeDtypeStruct((npair, 2 * gl * p, 2 * LANES), BF16),
        ),
        scratch_shapes=[pltpu.VMEM((g * h, p), F32), pltpu.VMEM((g * h, p), F32)],
        name="s5_prep",
    )(a_re, a_im, log_dt.reshape(g, 1),
      rows_gh(jnp.swapaxes(b_re, 1, 2)), rows_gh(jnp.swapaxes(b_im, 1, 2)), rows_gh(c_re), rows_gh(c_im))


def _mixer_kernel(h0_ref, hn_ref, gmix_ref, win_ref, rb_ref, rc_ref, at_ref, d_ref, wglu_ref, bglu_ref,
                  gs_ref, wp_ref, sc_ref, gp_ref, o_ref,
                  u_ref, ul_ref, s_ref, y_ref, yn_ref, st_ref, ext_ref, du_ref, xn_ref, *,
                  tc, nb, npair):
    c = pl.program_id(0)
    d = hn_ref.shape[-1]
    w = d // 2
    half = s_ref.shape[-1] // 2
    rows_per_t = 2 * nb
    nbuf = s_ref.shape[0]

    def in_proj(h_ref):
        h = h_ref[...].reshape(nb * tc, d)
        return jnp.dot(_rms(h, gmix_ref[...]).astype(BF16), win_ref[...], preferred_element_type=F32)

    @pl.when(c == 0)
    def _():
        ul_ref[...] = jnp.zeros(ul_ref.shape, F32)
        st_ref[...] = jnp.zeros(st_ref.shape, F32)
        ext_ref[:, :POOL_HALO, :] = jnp.zeros((nb, POOL_HALO, w), F32)
        u_ref[...] = in_proj(h0_ref)

    for jj in range(npair):
        for j2 in range(2):
            j = 2 * jj + j2
            for b in range(nb):
                ul_ref[jj, j2, pl.ds(b * 2 + j2, tc, stride=rows_per_t), :] = (
                    u_ref[b * tc:(b + 1) * tc, j * LANES:(j + 1) * LANES])
    du_ref[...] = d_ref[...] * u_ref[:, :w]
    for b in range(nb):
        ext_ref[b, POOL_HALO:, :] = u_ref[b * tc:(b + 1) * tc, w:]

    xn_ref[...] = _rms(hn_ref[...].reshape(nb * tc, d), gmix_ref[...]).astype(BF16)
    pw = d // npair
    gw = w // len(POOL_WINDOWS)
    t_idx = c * tc + lax.broadcasted_iota(jnp.int32, (tc, 1), 0)

    def next_u(p):
        u_ref[:, p * pw:(p + 1) * pw] = jnp.dot(
            xn_ref[...], win_ref[:, p * pw:(p + 1) * pw], preferred_element_type=F32)

    def b_proj(jj):
        lhs = jnp.concatenate([ul_ref[jj, 0], ul_ref[jj, 1]], axis=-1).astype(BF16)
        s_ref[jj % nbuf] = jnp.dot(lhs, rb_ref[jj], preferred_element_type=F32)

    def scan(jj):
        sb = jj % nbuf
        a_r = at_ref[jj, :, :half]
        a_i = at_ref[jj, :, half:]
        s_r = st_ref[jj, :, :half]
        s_i = st_ref[jj, :, half:]
        for t in range(tc):
            r0 = t * rows_per_t
            n_r = a_r * s_r - a_i * s_i + s_ref[sb, r0:r0 + rows_per_t, :half]
            n_i = a_r * s_i + a_i * s_r + s_ref[sb, r0:r0 + rows_per_t, half:]
            s_ref[sb, r0:r0 + rows_per_t, :half] = n_r
            s_ref[sb, r0:r0 + rows_per_t, half:] = n_i
            s_r, s_i = n_r, n_i
        st_ref[jj, :, :half] = s_r
        st_ref[jj, :, half:] = s_i

    def c_proj(jj):
        sb = jj % nbuf
        yy = jnp.dot(s_ref[sb].astype(BF16), rc_ref[jj], preferred_element_type=F32)
        y_ref[sb, 0] = yy[:, :LANES]
        y_ref[sb, 1] = yy[:, LANES:]
        for j2 in range(2):
            j = 2 * jj + j2
            for b in range(nb):
                yn_ref[b * tc:(b + 1) * tc, j * LANES:(j + 1) * LANES] = (
                    y_ref[sb, j2, pl.ds(b * 2 + j2, tc, stride=rows_per_t), :])

    def s5_out():
        y = yn_ref[...] + du_ref[...]
        y = jax.nn.gelu(y)
        z = jnp.dot(y.astype(BF16), wglu_ref[...], preferred_element_type=F32) + bglu_ref[...]
        y = y * jax.nn.sigmoid(z)
        o_ref[:, :, :w] = _rms(y, gs_ref[...]).astype(BF16).reshape(nb, tc, w)

    def pool(b):
        zs = []
        ssq = jnp.zeros((tc, 1), F32)
        for gi, win in enumerate(POOL_WINDOWS):
            e = ext_ref[b, :, gi * gw:(gi + 1) * gw]
            s = e
            shift = 1
            while shift < win:
                s = s + pltpu.roll(s, shift, axis=0)
                shift *= 2
            cnt = jnp.minimum(t_idx + 1, win).astype(F32)
            pooled = s[POOL_HALO:] / cnt - e[POOL_HALO:]
            z = jnp.dot(pooled.astype(BF16), wp_ref[gi], preferred_element_type=F32)
            z = z * sc_ref[:, gi * gw:(gi + 1) * gw]
            ssq = ssq + jnp.sum(z * z, axis=-1, keepdims=True)
            zs.append(z)
        inv = lax.rsqrt(ssq / w + EPS)
        for gi in range(len(POOL_WINDOWS)):
            o_ref[b, :, w + gi * gw:w + (gi + 1) * gw] = (
                zs[gi] * inv * gp_ref[:, gi * gw:(gi + 1) * gw]).astype(BF16)
        ext_ref[b, :POOL_HALO, :] = ext_ref[b, tc:, :]

    for jj in range(min(nbuf, npair)):
        b_proj(jj)
    for jj in range(npair):
        if jj < npair - 1:
            next_u(jj)
        scan(jj)
        for b in range(jj * nb // npair, (jj + 1) * nb // npair):
            pool(b)
        c_proj(jj)
        if jj + nbuf < npair:
            b_proj(jj + nbuf)
    next_u(npair - 1)
    s5_out()


def _mixer(h3, gmix, win, rb, rc, at, dskip, wglu, bglu, gs, wp, sc, gp):
    tc = TILES["mixer"]
    nb, L, d = h3.shape
    w = d // 2
    npair = rb.shape[0]
    nstate = rb.shape[2]
    rows = tc * 2 * nb
    nbuf = 2
    scratch = [
        pltpu.VMEM((nb * tc, d), F32),
        pltpu.VMEM((npair, 2, rows, LANES), F32),
        pltpu.VMEM((nbuf, rows, nstate), F32),
        pltpu.VMEM((nbuf, 2, rows, LANES), F32),
        pltpu.VMEM((nb * tc, w), F32),
        pltpu.VMEM((npair, 2 * nb, nstate), F32),
        pltpu.VMEM((nb, tc + POOL_HALO, w), F32),
        pltpu.VMEM((nb * tc, w), F32),
        pltpu.VMEM((nb * tc, d), BF16),
    ]
    blocks = (3 * nb * tc * d * 4 + 3 * nb * tc * d * 2 + d * d * 2 + rb.size * 2 + rc.size * 2
              + at.size * 4 + w * w * 2 + wp.size * 2
              + (nb * tc * d + npair * 2 * rows * LANES + nbuf * rows * nstate + nbuf * 2 * rows * LANES
                 + 2 * nb * tc * w + npair * 2 * nb * nstate + nb * (tc + POOL_HALO) * w) * 4)
    temps = nb * tc * d * 6 + rows * nstate * 6 + 4 * nb * tc * w * 4
    nchunks = L // tc
    return pl.pallas_call(
        functools.partial(_mixer_kernel, tc=tc, nb=nb, npair=npair),
        grid=(nchunks,),
        in_specs=[
            pl.BlockSpec((nb, tc, d), lambda c: (0, 0, 0), pipeline_mode=pl.Buffered(1)),
            pl.BlockSpec((nb, tc, d), lambda c: (0, jnp.minimum(c + 1, nchunks - 1), 0)),
            _const_spec((1, d)),
            _const_spec((d, d)),
            _const_spec(rb.shape),
            _const_spec(rc.shape),
            _const_spec(at.shape),
            _const_spec((1, w)),
            _const_spec((w, w)),
            _const_spec((1, w)),
            _const_spec((1, w)),
            _const_spec(wp.shape),
            _const_spec((1, w)),
            _const_spec((1, w)),
        ],
        out_specs=pl.BlockSpec((nb, tc, d), lambda c: (0, c, 0)),
        out_shape=jax.ShapeDtypeStruct((nb, L, d), BF16),
        scratch_shapes=scratch,
        compiler_params=_params(blocks, temps, 1),
        name="mixer",
    )(h3, h3, gmix, win, rb, rc, at, dskip, wglu, bglu, gs, wp, sc, gp)


def _kv_kernel(m_ref, g_ref, wk_ref, wv_ref, k_ref, v_ref, mn_ref):
    @pl.when(pl.program_id(0) == 0)
    def _():
        mn_ref[...] = _rms(m_ref[...], g_ref[...]).astype(BF16)

    mn = mn_ref[...]
    k_ref[...] = jnp.dot(mn, wk_ref[...], preferred_element_type=F32).astype(BF16)
    v_ref[...] = jnp.dot(mn, wv_ref[...], preferred_element_type=F32).astype(BF16)


def _kv(mem2, g, wk, wv):
    tn = TILES["kv"]
    n, d = mem2.shape
    blocks = n * d * 4 + n * d * 2 + 4 * d * tn * 2 + 4 * n * tn * 2
    temps = 2 * n * tn * 4 + n * d * 4
    return pl.pallas_call(
        _kv_kernel,
        grid=(d // tn,),
        in_specs=[
            _const_spec((n, d)),
            _const_spec((1, d)),
            pl.BlockSpec((d, tn), lambda j: (0, j)),
            pl.BlockSpec((d, tn), lambda j: (0, j)),
        ],
        out_specs=(pl.BlockSpec((n, tn), lambda j: (0, j)), pl.BlockSpec((n, tn), lambda j: (0, j))),
        out_shape=(jax.ShapeDtypeStruct((n, d), BF16), jax.ShapeDtypeStruct((n, d), BF16)),
        scratch_shapes=[pltpu.VMEM((n, d), BF16)],
        compiler_params=_params(blocks, temps, 1),
        name="mem_kv",
    )(mem2, g, wk, wv)


def _xattn_kernel(h_ref, m_ref, wout_ref, g_ref, wq_ref, k_ref, v_ref, wo_ref, o_ref):
    o_ref[...] = h_ref[...] + jnp.dot(m_ref[...], wout_ref[...], preferred_element_type=F32)
    h = o_ref[...]
    d = h.shape[-1]
    hd = d // MEM_HEADS
    q = jnp.dot(_rms(h, g_ref[...]).astype(BF16), wq_ref[...], preferred_element_type=F32)
    outs = []
    for hh in range(MEM_HEADS):
        qh = q[:, hh * hd:(hh + 1) * hd].astype(BF16)
        kh = k_ref[:, hh * hd:(hh + 1) * hd]
        s = lax.dot_general(qh, kh, (((1,), (1,)), ((), ())), preferred_element_type=F32)
        s = s * (hd ** -0.5)
        e = jnp.exp(s - jnp.max(s, axis=-1, keepdims=True))
        p = e / jnp.sum(e, axis=-1, keepdims=True)
        outs.append(jnp.dot(p.astype(BF16), v_ref[:, hh * hd:(hh + 1) * hd],
                            preferred_element_type=F32).astype(BF16))
    o = jnp.concatenate(outs, axis=-1)
    o_ref[...] = h + jnp.dot(o, wo_ref[...], preferred_element_type=F32)


def _xattn(h, m, wout, g, wq, k, v, wo, *, nb):
    tm = TILES["xattn"]
    n, d = h.shape
    nm = k.shape[0] // nb
    tiles = n // nb // tm
    rows = lambda b, i: (b * tiles + i, 0)
    blocks = 4 * tm * d * 4 + 2 * tm * d * 2 + 3 * d * d * 2 + 4 * nm * d * 2
    temps = 3 * tm * d * 4
    return pl.pallas_call(
        _xattn_kernel,
        grid=(nb, tiles),
        in_specs=[
            pl.BlockSpec((tm, d), rows),
            pl.BlockSpec((tm, d), rows),
            _const_spec((d, d)),
            _const_spec((1, d)),
            _const_spec((d, d)),
            pl.BlockSpec((nm, d), lambda b, i: (b, 0)),
            pl.BlockSpec((nm, d), lambda b, i: (b, 0)),
            _const_spec((d, d)),
        ],
        out_specs=pl.BlockSpec((tm, d), rows),
        out_shape=jax.ShapeDtypeStruct((n, d), F32),
        compiler_params=_params(blocks, temps, 2),
        name="xattn",
    )(h, m, wout, g, wq, k, v, wo)


def _s5_pole_table(abar_re, abar_im, npair, nb):
    g, p = abar_re.shape
    gl = g // (2 * npair)
    ab = jnp.stack([abar_re, abar_im]).reshape(2, npair, 2, gl, p)
    at = jnp.transpose(ab, (1, 2, 0, 3, 4)).reshape(npair, 1, 2, 2 * gl * p)
    return jnp.broadcast_to(at, (npair, nb, 2, 2 * gl * p)).reshape(npair, 2 * nb, 2 * gl * p)


def kernel(x, mem, g_ffn1, w1_gate, w1_up, w1_down, g_mix, w_in, ssm_a_re, ssm_a_im, ssm_log_dt,
           ssm_b_re, ssm_b_im, ssm_c_re, ssm_c_im, ssm_d, w_glu, b_glu, w_pool, pool_scale,
           g_out_ssm, g_out_pool, w_out, g_xattn, g_mem, w_q, w_k, w_v, w_o,
           g_ffn2, w2_gate, w2_up, w2_down, g_final):
    nb, L, d = x.shape
    n = nb * L
    depth = g_ffn1.shape[0]
    bf = lambda a: a.astype(BF16)
    row = lambda a: a.reshape(1, -1)

    h = x.reshape(n, d)
    for l in range(depth):
        last = l == depth - 1
        pool_shape = w_pool[l].shape
        later = [w2_down[l], w_in[l], w_out[l], w_q[l], w_k[l], w_v[l], w_o[l],
                 w_glu[l], w_pool[l].reshape(-1, pool_shape[-1])]
        head, w1g, w1u, w1d = _ffn_head(h, row(g_ffn1[l]), w1_gate[l], w1_up[l], w1_down[l])
        h, later = _ffn(h, row(g_ffn1[l]), w1g, w1u, w1d, row(g_final), final_norm=False,
                        side=later, side_tiled=[w2_gate[l], w2_up[l]], head=head)
        w2d, win, wout, wq, wk, wv, wo, wglu, wpool, w2g, w2u = later

        abar_re, abar_im, rb, rc = _s5_prep(ssm_a_re[l], ssm_a_im[l], ssm_log_dt[l],
                                            ssm_b_re[l], ssm_b_im[l], ssm_c_re[l], ssm_c_im[l])
        at = _s5_pole_table(abar_re, abar_im, rb.shape[0], nb)
        merged = _mixer(h.reshape(nb, L, d), row(g_mix[l]), win, rb, rc, at, row(ssm_d[l]),
                        wglu, row(b_glu[l]), row(g_out_ssm[l]),
                        wpool.reshape(pool_shape), row(pool_scale[l]), row(g_out_pool[l]))
        k, v = _kv(mem.reshape(nb * mem.shape[1], d), row(g_mem[l]), wk, wv)
        h = _xattn(h, merged.reshape(n, d), wout, row(g_xattn[l]), wq, k, v, wo, nb=nb)

        h, _ = _ffn(h, row(g_ffn2[l]), w2g, w2u, w2d, row(g_final), final_norm=last)
    return h.reshape(nb, L, d)
```

```python
import functools

import jax
import jax.numpy as jnp
from jax import lax
from jax.experimental import pallas as pl
from jax.experimental.pallas import tpu as pltpu

F32 = jnp.float32
BF16 = jnp.bfloat16

EPS = 1e-6
POOL_WINDOWS = (2, 4, 8, 16)
MEM_HEADS = 4

LANES = 128
SUBLANES = 8
MXU_COLS = 256
VMEM_BYTES_V7X = 64 * 1024 * 1024
VMEM_RESERVE = 2 * 1024 * 1024
POOL_HALO = 16

TILES = dict(ffn=(1024, 512), ffn_head=(1024, 256), mixer=128, xattn=512, kv=512)


def _params(block_bytes, temp_bytes, ndims):
    return pltpu.CompilerParams(
        dimension_semantics=("arbitrary",) * ndims,
        vmem_limit_bytes=int(min(VMEM_BYTES_V7X - VMEM_RESERVE, block_bytes + temp_bytes)),
    )


def _rms(x, g):
    return x * lax.rsqrt(jnp.mean(x * x, axis=-1, keepdims=True) + EPS) * g


def _const_spec(shape):
    nd = len(shape)
    return pl.BlockSpec(shape, lambda *_: (0,) * nd, pipeline_mode=pl.Buffered(1))


def _ffn_step(k, nk, x_ref, g_ref, wg_ref, wu_ref, wd_ref, o_ref, xn_ref, gf_ref=None):
    def body(first, last=False):
        if first:
            xn_ref[...] = _rms(x_ref[...], g_ref[...]).astype(BF16)
        xn = xn_ref[...]
        npiece, _, hf = wg_ref.shape
        acts = []
        for p in range(npiece):
            gate = jnp.dot(xn, wg_ref[p], preferred_element_type=F32)
            up = jnp.dot(xn, wu_ref[p], preferred_element_type=F32)
            acts.append((jax.nn.silu(gate) * up * 0.5).astype(BF16))
        acc = None
        for p, act in enumerate(acts):
            part = jnp.dot(act, wd_ref[p * hf:(p + 1) * hf, :], preferred_element_type=F32)
            acc = part if acc is None else acc + part
        o = (x_ref[...] if first else o_ref[...]) + acc
        o_ref[...] = _rms(o, gf_ref[...]) if last else o

    pl.when(k == 0)(functools.partial(body, True))
    if gf_ref is None:
        pl.when(k > 0)(functools.partial(body, False))
    else:
        pl.when((k > 0) & (k < nk - 1))(functools.partial(body, False))
        pl.when(k == nk - 1)(functools.partial(body, False, True))


def _ffn_kernel(*refs, final_norm, n_side, adopt):
    x_ref, g_ref, wg_ref, wu_ref, wd_ref, gf_ref = refs[:6]
    n_in = 6 + adopt
    side_in = refs[n_in:n_in + n_side]
    o_ref = refs[n_in + n_side]
    side_out = refs[n_in + n_side + 1:n_in + 2 * n_side + 1]
    xn_ref = refs[n_in + 2 * n_side + 1]
    i = pl.program_id(0)
    k = pl.program_id(1)
    nk = pl.num_programs(1)
    gf = gf_ref if final_norm else None

    if adopt:
        head_ref, sem = refs[6], refs[n_in + 2 * n_side + 2]

        @pl.when((i == 0) & (k == 0))
        def _():
            cp = pltpu.make_async_copy(head_ref, o_ref, sem)
            cp.start()
            cp.wait()

        @pl.when(i > 0)
        def _():
            _ffn_step(k, nk, x_ref, g_ref, wg_ref, wu_ref, wd_ref, o_ref, xn_ref, gf)
    else:
        _ffn_step(k, nk, x_ref, g_ref, wg_ref, wu_ref, wd_ref, o_ref, xn_ref, gf)

    for src, dst in zip(side_in, side_out):
        if len(dst.shape) == 3:
            for q in range(dst.shape[0]):
                dst[q] = src[:, q * MXU_COLS:(q + 1) * MXU_COLS].astype(BF16)
        else:
            dst[...] = src[...].astype(BF16)


def _ffn_head_kernel(x_ref, g_ref, wg_ref, wu_ref, wd_ref, o_ref, wgb_ref, wub_ref, wdb_ref, xn_ref):
    wgb_ref[0] = wg_ref[...].astype(BF16)
    wub_ref[0] = wu_ref[...].astype(BF16)
    wdb_ref[...] = wd_ref[...].astype(BF16)
    _ffn_step(pl.program_id(0), pl.num_programs(0), x_ref, g_ref, wgb_ref, wub_ref, wdb_ref, o_ref, xn_ref)


def _ffn_head(x, g, wg, wu, wd):
    tm, tf = TILES["ffn_head"]
    assert tf == MXU_COLS
    d = x.shape[1]
    dff = wg.shape[1]
    blocks = 2 * (tm * d * 4) * 2 + tm * d * 2 + 2 * 3 * d * tf * (4 + 2)
    temps = 3 * tm * tf * 4 + tm * d * 4
    col = lambda k: (0, k)
    row = lambda k: (k, 0)
    tile = lambda k: (k, 0, 0)
    return pl.pallas_call(
        _ffn_head_kernel,
        grid=(dff // tf,),
        in_specs=[
            pl.BlockSpec((tm, d), lambda k: (0, 0)),
            _const_spec((1, d)),
            pl.BlockSpec((d, tf), col),
            pl.BlockSpec((d, tf), col),
            pl.BlockSpec((tf, d), row),
        ],
        out_specs=[
            pl.BlockSpec((tm, d), lambda k: (0, 0)),
            pl.BlockSpec((1, d, tf), tile),
            pl.BlockSpec((1, d, tf), tile),
            pl.BlockSpec((tf, d), row),
        ],
        out_shape=[
            jax.ShapeDtypeStruct((tm, d), F32),
            jax.ShapeDtypeStruct((dff // tf, d, tf), BF16),
            jax.ShapeDtypeStruct((dff // tf, d, tf), BF16),
            jax.ShapeDtypeStruct(wd.shape, BF16),
        ],
        scratch_shapes=[pltpu.VMEM((tm, d), BF16)],
        compiler_params=_params(blocks, temps, 1),
        name="ffn_head",
    )(x, g, wg, wu, wd)


def _cast_spec(shape, ni, nk):
    r, c = shape
    row_align = 2 * SUBLANES
    if c % (ni * LANES) == 0 and r % (nk * row_align) == 0:
        return pl.BlockSpec((r // nk, c // ni), lambda i, k: (k, i))
    m = max(m for m in range(1, nk + 1) if c % (m * LANES) == 0)
    assert r % (ni * row_align) == 0, shape
    return pl.BlockSpec((r // ni, c // m), lambda i, k: (i, jnp.minimum(k, m - 1)))


def _ffn(x, g, wg, wu, wd, gf, *, final_norm, side=(), side_tiled=(), head=None):
    tm, tf = TILES["ffn"]
    n, d = x.shape
    dff = wd.shape[0]
    ni, nk = n // tm, dff // tf
    pieces = tf // MXU_COLS
    adopt = head is not None
    if adopt:
        assert head.shape == (tm, d) and TILES["ffn_head"][0] == tm
        xi = lambda i: jnp.maximum(i, 1)
        wk = lambda i, k: jnp.where(i == 0, 0, k)
    else:
        xi = lambda i: i
        wk = lambda i, k: k
    side_specs = [_cast_spec(w.shape, ni, nk) for w in side]
    side_out_specs = list(side_specs)
    side_shapes = [jax.ShapeDtypeStruct(w.shape, BF16) for w in side]
    for w in side_tiled:
        r, c = w.shape
        assert c == dff and r % (ni * 2 * SUBLANES) == 0
        side_specs.append(pl.BlockSpec((r // ni, tf), lambda i, k: (i, k)))
        side_out_specs.append(pl.BlockSpec((pieces, r // ni, MXU_COLS), lambda i, k: (k, i, 0)))
        side_shapes.append(jax.ShapeDtypeStruct((c // MXU_COLS, r, MXU_COLS), BF16))
    n_side = len(side_specs)
    side_bytes = sum(2 * sp.block_shape[0] * sp.block_shape[1] * (4 + 2) for sp in side_specs)
    blocks = 2 * (tm * d * 4) * 2 + tm * d * 2 + 2 * 3 * d * tf * 2 + side_bytes
    temps = 3 * tm * tf * 4 + tm * d * 4
    outs = pl.pallas_call(
        functools.partial(_ffn_kernel, final_norm=final_norm, n_side=n_side, adopt=adopt),
        grid=(ni, nk),
        in_specs=[
            pl.BlockSpec((tm, d), lambda i, k: (xi(i), 0)),
            _const_spec((1, d)),
            pl.BlockSpec((pieces, d, MXU_COLS), lambda i, k: (wk(i, k), 0, 0)),
            pl.BlockSpec((pieces, d, MXU_COLS), lambda i, k: (wk(i, k), 0, 0)),
            pl.BlockSpec((tf, d), lambda i, k: (wk(i, k), 0)),
            _const_spec((1, d)),
        ] + ([pl.BlockSpec(memory_space=pl.ANY)] if adopt else []) + side_specs,
        out_specs=[pl.BlockSpec((tm, d), lambda i, k: (i, 0))] + side_out_specs,
        out_shape=[jax.ShapeDtypeStruct((n, d), F32)] + side_shapes,
        scratch_shapes=[pltpu.VMEM((tm, d), BF16)] + ([pltpu.SemaphoreType.DMA(())] if adopt else []),
        compiler_params=_params(blocks, temps, 2),
        name="ffn",
    )(x, g, wg, wu, wd, gf, *([head] if adopt else []), *side, *side_tiled)
    return outs[0], list(outs[1:])


def _s5_prep_kernel(lr_ref, li_ref, ldt_ref, br_ref, bi_ref, cr_ref, ci_ref,
                    ar_ref, ai_ref, rb_ref, rc_ref, bbr_ref, bbi_ref, *, h):
    lr = lr_ref[...]
    li = li_ref[...]
    g, p = lr.shape
    dt = jnp.exp(ldt_ref[...])
    mag = jnp.exp(lr * dt)
    abar_re = mag * jnp.cos(li * dt)
    abar_im = mag * jnp.sin(li * dt)
    nr, ni = abar_re - 1.0, abar_im
    den = lr * lr + li * li
    fr = (nr * lr + ni * li) / den
    fi = (ni * lr - nr * li) / den
    ar_ref[...] = abar_re
    ai_ref[...] = abar_im
    for gg in range(g):
        rows = slice(gg * h, (gg + 1) * h)
        frg, fig = fr[gg:gg + 1, :], fi[gg:gg + 1, :]
        br, bi = br_ref[rows, :], bi_ref[rows, :]
        bbr_ref[rows, :] = frg * br - fig * bi
        bbi_ref[rows, :] = frg * bi + fig * br

    gl = LANES // h
    half = gl * p
    iota = lax.broadcasted_iota
    spread = (iota(jnp.int32, (p, half), 1) % p == iota(jnp.int32, (p, half), 0)).astype(BF16)
    spread_t = (iota(jnp.int32, (half, p), 0) % p == iota(jnp.int32, (half, p), 1)).astype(BF16)
    diag_b = iota(jnp.int32, (LANES, half), 0) // h == iota(jnp.int32, (LANES, half), 1) // p
    diag_c = iota(jnp.int32, (half, LANES), 0) // p == iota(jnp.int32, (half, LANES), 1) // h
    for j in range(g // gl):
        jj, j2 = divmod(j, 2)
        rows = slice(j * LANES, (j + 1) * LANES)
        for ri, (b_ref, c_ref, sign) in enumerate(((bbr_ref, cr_ref, 1.0), (bbi_ref, ci_ref, -1.0))):
            blk = jnp.dot(b_ref[rows, :].astype(BF16), spread, preferred_element_type=F32)
            rb_ref[jj, j2 * LANES:(j2 + 1) * LANES, ri * half:(ri + 1) * half] = (
                jnp.where(diag_b, blk, 0.0).astype(BF16))
            blk = lax.dot_general(spread_t, c_ref[rows, :].astype(BF16), (((1,), (1,)), ((), ())),
                                  preferred_element_type=F32)
            rc_ref[jj, ri * half:(ri + 1) * half, j2 * LANES:(j2 + 1) * LANES] = (
                jnp.where(diag_c, sign * blk, 0.0).astype(BF16))


def _s5_prep(a_re, a_im, log_dt, b_re, b_im, c_re, c_im):
    g, p, h = b_re.shape
    gl = LANES // h
    npair = g // (2 * gl)
    rows_gh = lambda a: a.reshape(g * h, p)
    return pl.pallas_call(
        functools.partial(_s5_prep_kernel, h=h),
        out_shape=(
            jax.ShapeDtypeStruct((g, p), F32),
            jax.ShapeDtypeStruct((g, p), F32),
            jax.ShapeDtypeStruct((npair, 2 * LANES, 2 * gl * p), BF16),
            jax.ShapeDtypeStruct((npair, 2 * gl * p, 2 * LANES), BF16),
        ),
        scratch_shapes=[pltpu.VMEM((g * h, p), F32), pltpu.VMEM((g * h, p), F32)],
        name="s5_prep",
    )(a_re, a_im, log_dt.reshape(g, 1),
      rows_gh(jnp.swapaxes(b_re, 1, 2)), rows_gh(jnp.swapaxes(b_im, 1, 2)), rows_gh(c_re), rows_gh(c_im))


def _mixer_kernel(h0_ref, hn_ref, gmix_ref, win_ref, rb_ref, rc_ref, at_ref, d_ref, wglu_ref, bglu_ref,
                  gs_ref, wp_ref, sc_ref, gp_ref, o_ref,
                  u_ref, ul_ref, s_ref, y_ref, yn_ref, st_ref, ext_ref, du_ref, xn_ref, *,
                  tc, nb, npair):
    c = pl.program_id(0)
    d = hn_ref.shape[-1]
    w = d // 2
    half = s_ref.shape[-1] // 2
    rows_per_t = 2 * nb
    nbuf = s_ref.shape[0]

    def in_proj(h_ref):
        h = h_ref[...].reshape(nb * tc, d)
        return jnp.dot(_rms(h, gmix_ref[...]).astype(BF16), win_ref[...], preferred_element_type=F32)

    @pl.when(c == 0)
    def _():
        ul_ref[...] = jnp.zeros(ul_ref.shape, F32)
        st_ref[...] = jnp.zeros(st_ref.shape, F32)
        ext_ref[:, :POOL_HALO, :] = jnp.zeros((nb, POOL_HALO, w), F32)
        u_ref[...] = in_proj(h0_ref)

    for jj in range(npair):
        for j2 in range(2):
            j = 2 * jj + j2
            for b in range(nb):
                ul_ref[jj, j2, pl.ds(b * 2 + j2, tc, stride=rows_per_t), :] = (
                    u_ref[b * tc:(b + 1) * tc, j * LANES:(j + 1) * LANES])
    du_ref[...] = d_ref[...] * u_ref[:, :w]
    for b in range(nb):
        ext_ref[b, POOL_HALO:, :] = u_ref[b * tc:(b + 1) * tc, w:]

    xn_ref[...] = _rms(hn_ref[...].reshape(nb * tc, d), gmix_ref[...]).astype(BF16)
    pw = d // (2 * npair)
    gw = w // len(POOL_WINDOWS)
    t_idx = c * tc + lax.broadcasted_iota(jnp.int32, (tc, 1), 0)

    def next_u(p):
        u_ref[:, p * pw:(p + 1) * pw] = jnp.dot(
            xn_ref[...], win_ref[:, p * pw:(p + 1) * pw], preferred_element_type=F32)

    def b_proj(jj):
        lhs = jnp.concatenate([ul_ref[jj, 0], ul_ref[jj, 1]], axis=-1).astype(BF16)
        s_ref[jj % nbuf] = jnp.dot(lhs, rb_ref[jj], preferred_element_type=F32)

    def scan(jj):
        sb = jj % nbuf
        a_r = at_ref[jj, :, :half]
        a_i = at_ref[jj, :, half:]
        s_r = st_ref[jj, :, :half]
        s_i = st_ref[jj, :, half:]
        for t in range(tc):
            r0 = t * rows_per_t
            n_r = a_r * s_r - a_i * s_i + s_ref[sb, r0:r0 + rows_per_t, :half]
            n_i = a_r * s_i + a_i * s_r + s_ref[sb, r0:r0 + rows_per_t, half:]
            s_ref[sb, r0:r0 + rows_per_t, :half] = n_r
            s_ref[sb, r0:r0 + rows_per_t, half:] = n_i
            s_r, s_i = n_r, n_i
        st_ref[jj, :, :half] = s_r
        st_ref[jj, :, half:] = s_i

    def c_proj(jj):
        sb = jj % nbuf
        yy = jnp.dot(s_ref[sb].astype(BF16), rc_ref[jj], preferred_element_type=F32)
        y_ref[sb, 0] = yy[:, :LANES]
        y_ref[sb, 1] = yy[:, LANES:]
        for j2 in range(2):
            j = 2 * jj + j2
            for b in range(nb):
                yn_ref[b * tc:(b + 1) * tc, j * LANES:(j + 1) * LANES] = (
                    y_ref[sb, j2, pl.ds(b * 2 + j2, tc, stride=rows_per_t), :])

    def s5_out():
        y = yn_ref[...] + du_ref[...]
        y = jax.nn.gelu(y)
        z = jnp.dot(y.astype(BF16), wglu_ref[...], preferred_element_type=F32) + bglu_ref[...]
        y = y * jax.nn.sigmoid(z)
        o_ref[:, :, :w] = _rms(y, gs_ref[...]).astype(BF16).reshape(nb, tc, w)

    def pool(b):
        zs = []
        ssq = jnp.zeros((tc, 1), F32)
        for gi, win in enumerate(POOL_WINDOWS):
            e = ext_ref[b, :, gi * gw:(gi + 1) * gw]
            s = e
            shift = 1
            while shift < win:
                s = s + pltpu.roll(s, shift, axis=0)
                shift *= 2
            cnt = jnp.minimum(t_idx + 1, win).astype(F32)
            pooled = s[POOL_HALO:] / cnt - e[POOL_HALO:]
            z = jnp.dot(pooled.astype(BF16), wp_ref[gi], preferred_element_type=F32)
            z = z * sc_ref[:, gi * gw:(gi + 1) * gw]
            ssq = ssq + jnp.sum(z * z, axis=-1, keepdims=True)
            zs.append(z)
        inv = lax.rsqrt(ssq / w + EPS)
        for gi in range(len(POOL_WINDOWS)):
            o_ref[b, :, w + gi * gw:w + (gi + 1) * gw] = (
                zs[gi] * inv * gp_ref[:, gi * gw:(gi + 1) * gw]).astype(BF16)
        ext_ref[b, :POOL_HALO, :] = ext_ref[b, tc:, :]

    for jj in range(min(nbuf, npair)):
        b_proj(jj)
    for jj in range(npair):
        next_u(2 * jj)
        scan(jj)
        for b in range(jj * nb // npair, (jj + 1) * nb // npair):
            pool(b)
        c_proj(jj)
        if jj < npair - 1:
            next_u(2 * jj + 1)
        if jj + nbuf < npair:
            b_proj(jj + nbuf)
    next_u(2 * npair - 1)
    s5_out()


def _mixer(h3, gmix, win, rb, rc, at, dskip, wglu, bglu, gs, wp, sc, gp):
    tc = TILES["mixer"]
    nb, L, d = h3.shape
    w = d // 2
    npair = rb.shape[0]
    nstate = rb.shape[2]
    rows = tc * 2 * nb
    nbuf = 2
    scratch = [
        pltpu.VMEM((nb * tc, d), F32),
        pltpu.VMEM((npair, 2, rows, LANES), F32),
        pltpu.VMEM((nbuf, rows, nstate), F32),
        pltpu.VMEM((nbuf, 2, rows, LANES), F32),
        pltpu.VMEM((nb * tc, w), F32),
        pltpu.VMEM((npair, 2 * nb, nstate), F32),
        pltpu.VMEM((nb, tc + POOL_HALO, w), F32),
        pltpu.VMEM((nb * tc, w), F32),
        pltpu.VMEM((nb * tc, d), BF16),
    ]
    blocks = (3 * nb * tc * d * 4 + 3 * nb * tc * d * 2 + d * d * 2 + rb.size * 2 + rc.size * 2
              + at.size * 4 + w * w * 2 + wp.size * 2
              + (nb * tc * d + npair * 2 * rows * LANES + nbuf * rows * nstate + nbuf * 2 * rows * LANES
                 + 2 * nb * tc * w + npair * 2 * nb * nstate + nb * (tc + POOL_HALO) * w) * 4)
    temps = nb * tc * d * 6 + rows * nstate * 6 + 4 * nb * tc * w * 4
    nchunks = L // tc
    return pl.pallas_call(
        functools.partial(_mixer_kernel, tc=tc, nb=nb, npair=npair),
        grid=(nchunks,),
        in_specs=[
            pl.BlockSpec((nb, tc, d), lambda c: (0, 0, 0), pipeline_mode=pl.Buffered(1)),
            pl.BlockSpec((nb, tc, d), lambda c: (0, jnp.minimum(c + 1, nchunks - 1), 0)),
            _const_spec((1, d)),
            _const_spec((d, d)),
            _const_spec(rb.shape),
            _const_spec(rc.shape),
            _const_spec(at.shape),
            _const_spec((1, w)),
            _const_spec((w, w)),
            _const_spec((1, w)),
            _const_spec((1, w)),
            _const_spec(wp.shape),
            _const_spec((1, w)),
            _const_spec((1, w)),
        ],
        out_specs=pl.BlockSpec((nb, tc, d), lambda c: (0, c, 0)),
        out_shape=jax.ShapeDtypeStruct((nb, L, d), BF16),
        scratch_shapes=scratch,
        compiler_params=_params(blocks, temps, 1),
        name="mixer",
    )(h3, h3, gmix, win, rb, rc, at, dskip, wglu, bglu, gs, wp, sc, gp)


def _kv_kernel(m_ref, g_ref, wk_ref, wv_ref, k_ref, v_ref, mn_ref):
    @pl.when(pl.program_id(0) == 0)
    def _():
        mn_ref[...] = _rms(m_ref[...], g_ref[...]).astype(BF16)

    mn = mn_ref[...]
    k_ref[...] = jnp.dot(mn, wk_ref[...], preferred_element_type=F32).astype(BF16)
    v_ref[...] = jnp.dot(mn, wv_ref[...], preferred_element_type=F32).astype(BF16)


def _kv(mem2, g, wk, wv):
    tn = TILES["kv"]
    n, d = mem2.shape
    blocks = n * d * 4 + n * d * 2 + 4 * d * tn * 2 + 4 * n * tn * 2
    temps = 2 * n * tn * 4 + n * d * 4
    return pl.pallas_call(
        _kv_kernel,
        grid=(d // tn,),
        in_specs=[
            _const_spec((n, d)),
            _const_spec((1, d)),
            pl.BlockSpec((d, tn), lambda j: (0, j)),
            pl.BlockSpec((d, tn), lambda j: (0, j)),
        ],
        out_specs=(pl.BlockSpec((n, tn), lambda j: (0, j)), pl.BlockSpec((n, tn), lambda j: (0, j))),
        out_shape=(jax.ShapeDtypeStruct((n, d), BF16), jax.ShapeDtypeStruct((n, d), BF16)),
        scratch_shapes=[pltpu.VMEM((n, d), BF16)],
        compiler_params=_params(blocks, temps, 1),
        name="mem_kv",
    )(mem2, g, wk, wv)


def _xattn_kernel(h_ref, m_ref, wout_ref, g_ref, wq_ref, k_ref, v_ref, wo_ref, o_ref):
    o_ref[...] = h_ref[...] + jnp.dot(m_ref[...], wout_ref[...], preferred_element_type=F32)
    h = o_ref[...]
    d = h.shape[-1]
    hd = d // MEM_HEADS
    q = jnp.dot(_rms(h, g_ref[...]).astype(BF16), wq_ref[...], preferred_element_type=F32)
    outs = []
    for hh in range(MEM_HEADS):
        qh = q[:, hh * hd:(hh + 1) * hd].astype(BF16)
        kh = k_ref[:, hh * hd:(hh + 1) * hd]
        s = lax.dot_general(qh, kh, (((1,), (1,)), ((), ())), preferred_element_type=F32)
        s = s * (hd ** -0.5)
        e = jnp.exp(s - jnp.max(s, axis=-1, keepdims=True))
        p = e / jnp.sum(e, axis=-1, keepdims=True)
        outs.append(jnp.dot(p.astype(BF16), v_ref[:, hh * hd:(hh + 1) * hd],
                            preferred_element_type=F32).astype(BF16))
    o = jnp.concatenate(outs, axis=-1)
    o_ref[...] = h + jnp.dot(o, wo_ref[...], preferred_element_type=F32)


def _xattn(h, m, wout, g, wq, k, v, wo, *, nb):
    tm = TILES["xattn"]
    n, d = h.shape
    nm = k.shape[0] // nb
    tiles = n // nb // tm
    rows = lambda b, i: (b * tiles + i, 0)
    blocks = 4 * tm * d * 4 + 2 * tm * d * 2 + 3 * d * d * 2 + 4 * nm * d * 2
    temps = 3 * tm * d * 4
    return pl.pallas_call(
        _xattn_kernel,
        grid=(nb, tiles),
        in_specs=[
            pl.BlockSpec((tm, d), rows),
            pl.BlockSpec((tm, d), rows),
            _const_spec((d, d)),
            _const_spec((1, d)),
            _const_spec((d, d)),
            pl.BlockSpec((nm, d), lambda b, i: (b, 0)),
            pl.BlockSpec((nm, d), lambda b, i: (b, 0)),
            _const_spec((d, d)),
        ],
        out_specs=pl.BlockSpec((tm, d), rows),
        out_shape=jax.ShapeDtypeStruct((n, d), F32),
        compiler_params=_params(blocks, temps, 2),
        name="xattn",
    )(h, m, wout, g, wq, k, v, wo)


def _s5_pole_table(abar_re, abar_im, npair, nb):
    g, p = abar_re.shape
    gl = g // (2 * npair)
    ab = jnp.stack([abar_re, abar_im]).reshape(2, npair, 2, gl, p)
    at = jnp.transpose(ab, (1, 2, 0, 3, 4)).reshape(npair, 1, 2, 2 * gl * p)
    return jnp.broadcast_to(at, (npair, nb, 2, 2 * gl * p)).reshape(npair, 2 * nb, 2 * gl * p)


def kernel(x, mem, g_ffn1, w1_gate, w1_up, w1_down, g_mix, w_in, ssm_a_re, ssm_a_im, ssm_log_dt,
           ssm_b_re, ssm_b_im, ssm_c_re, ssm_c_im, ssm_d, w_glu, b_glu, w_pool, pool_scale,
           g_out_ssm, g_out_pool, w_out, g_xattn, g_mem, w_q, w_k, w_v, w_o,
           g_ffn2, w2_gate, w2_up, w2_down, g_final):
    nb, L, d = x.shape
    n = nb * L
    depth = g_ffn1.shape[0]
    bf = lambda a: a.astype(BF16)
    row = lambda a: a.reshape(1, -1)

    h = x.reshape(n, d)
    for l in range(depth):
        last = l == depth - 1
        pool_shape = w_pool[l].shape
        later = [w2_down[l], w_in[l], w_out[l], w_q[l], w_k[l], w_v[l], w_o[l],
                 w_glu[l], w_pool[l].reshape(-1, pool_shape[-1])]
        head, w1g, w1u, w1d = _ffn_head(h, row(g_ffn1[l]), w1_gate[l], w1_up[l], w1_down[l])
        h, later = _ffn(h, row(g_ffn1[l]), w1g, w1u, w1d, row(g_final), final_norm=False,
                        side=later, side_tiled=[w2_gate[l], w2_up[l]], head=head)
        w2d, win, wout, wq, wk, wv, wo, wglu, wpool, w2g, w2u = later

        abar_re, abar_im, rb, rc = _s5_prep(ssm_a_re[l], ssm_a_im[l], ssm_log_dt[l],
                                            ssm_b_re[l], ssm_b_im[l], ssm_c_re[l], ssm_c_im[l])
        at = _s5_pole_table(abar_re, abar_im, rb.shape[0], nb)
        merged = _mixer(h.reshape(nb, L, d), row(g_mix[l]), win, rb, rc, at, row(ssm_d[l]),
                        wglu, row(b_glu[l]), row(g_out_ssm[l]),
                        wpool.reshape(pool_shape), row(pool_scale[l]), row(g_out_pool[l]))
        k, v = _kv(mem.reshape(nb * mem.shape[1], d), row(g_mem[l]), wk, wv)
        h = _xattn(h, merged.reshape(n, d), wout, row(g_xattn[l]), wq, k, v, wo, nb=nb)

        h, _ = _ffn(h, row(g_ffn2[l]), w2g, w2u, w2d, row(g_final), final_norm=last)
    return h.reshape(nb, L, d)
```

```python
import functools

import jax
import jax.numpy as jnp
from jax import lax
from jax.experimental import pallas as pl
from jax.experimental.pallas import tpu as pltpu

F32 = jnp.float32
BF16 = jnp.bfloat16

EPS = 1e-6
POOL_WINDOWS = (2, 4, 8, 16)
MEM_HEADS = 4

LANES = 128
SUBLANES = 8
MXU_COLS = 256
VMEM_BYTES_V7X = 64 * 1024 * 1024
VMEM_RESERVE = 2 * 1024 * 1024
POOL_HALO = 16

TILES = dict(ffn=(1024, 512), ffn_head=(1024, 256), mixer=128, xattn=512, kv=512)


def _params(block_bytes, temp_bytes, ndims):
    return pltpu.CompilerParams(
        dimension_semantics=("arbitrary",) * ndims,
        vmem_limit_bytes=int(min(VMEM_BYTES_V7X - VMEM_RESERVE, block_bytes + temp_bytes)),
    )


def _rms(x, g):
    return x * lax.rsqrt(jnp.mean(x * x, axis=-1, keepdims=True) + EPS) * g


def _const_spec(shape):
    nd = len(shape)
    return pl.BlockSpec(shape, lambda *_: (0,) * nd, pipeline_mode=pl.Buffered(1))


def _ffn_step(k, nk, x_ref, g_ref, wg_ref, wu_ref, wd_ref, o_ref, xn_ref, gf_ref=None):
    def body(first, last=False):
        if first:
            xn_ref[...] = _rms(x_ref[...], g_ref[...]).astype(BF16)
        xn = xn_ref[...]
        npiece, _, hf = wg_ref.shape
        acts = []
        for p in range(npiece):
            gate = jnp.dot(xn, wg_ref[p], preferred_element_type=F32)
            up = jnp.dot(xn, wu_ref[p], preferred_element_type=F32)
            acts.append((jax.nn.silu(gate) * up * 0.5).astype(BF16))
        acc = None
        for p, act in enumerate(acts):
            part = jnp.dot(act, wd_ref[p * hf:(p + 1) * hf, :], preferred_element_type=F32)
            acc = part if acc is None else acc + part
        o = (x_ref[...] if first else o_ref[...]) + acc
        o_ref[...] = _rms(o, gf_ref[...]) if last else o

    pl.when(k == 0)(functools.partial(body, True))
    if gf_ref is None:
        pl.when(k > 0)(functools.partial(body, False))
    else:
        pl.when((k > 0) & (k < nk - 1))(functools.partial(body, False))
        pl.when(k == nk - 1)(functools.partial(body, False, True))


def _ffn_kernel(*refs, final_norm, n_side, adopt):
    x_ref, g_ref, wg_ref, wu_ref, wd_ref, gf_ref = refs[:6]
    n_in = 6 + adopt
    side_in = refs[n_in:n_in + n_side]
    o_ref = refs[n_in + n_side]
    side_out = refs[n_in + n_side + 1:n_in + 2 * n_side + 1]
    xn_ref = refs[n_in + 2 * n_side + 1]
    i = pl.program_id(0)
    k = pl.program_id(1)
    nk = pl.num_programs(1)
    gf = gf_ref if final_norm else None

    if adopt:
        head_ref, sem = refs[6], refs[n_in + 2 * n_side + 2]

        @pl.when((i == 0) & (k == 0))
        def _():
            cp = pltpu.make_async_copy(head_ref, o_ref, sem)
            cp.start()
            cp.wait()

        @pl.when(i > 0)
        def _():
            _ffn_step(k, nk, x_ref, g_ref, wg_ref, wu_ref, wd_ref, o_ref, xn_ref, gf)
    else:
        _ffn_step(k, nk, x_ref, g_ref, wg_ref, wu_ref, wd_ref, o_ref, xn_ref, gf)

    for src, dst in zip(side_in, side_out):
        if len(dst.shape) == 3:
            for q in range(dst.shape[0]):
                dst[q] = src[:, q * MXU_COLS:(q + 1) * MXU_COLS].astype(BF16)
        else:
            dst[...] = src[...].astype(BF16)


HEAD_SLOTS = 3


def _ffn_head_kernel(x_ref, g_ref, wg_hbm, wu_hbm, wd_hbm, o_ref, wgb_ref, wub_ref, wdb_ref,
                     xn_ref, wgf_ref, wuf_ref, wdf_ref, sem):
    k = pl.program_id(0)
    nk = pl.num_programs(0)
    tf = wgf_ref.shape[2]

    def tile_copies(kk):
        slot = lax.rem(kk, HEAD_SLOTS)
        cols = pl.ds(pl.multiple_of(kk * tf, tf), tf)
        return (pltpu.make_async_copy(wg_hbm.at[:, cols], wgf_ref.at[slot], sem.at[0, slot]),
                pltpu.make_async_copy(wu_hbm.at[:, cols], wuf_ref.at[slot], sem.at[1, slot]),
                pltpu.make_async_copy(wd_hbm.at[cols, :], wdf_ref.at[slot], sem.at[2, slot]))

    @pl.when(k == 0)
    def _():
        for kk in range(HEAD_SLOTS - 1):
            for cp in tile_copies(kk):
                cp.start()

    @pl.when(k + HEAD_SLOTS - 1 < nk)
    def _():
        for cp in tile_copies(k + HEAD_SLOTS - 1):
            cp.start()

    for cp in tile_copies(k):
        cp.wait()
    slot = lax.rem(k, HEAD_SLOTS)
    wgb_ref[0] = wgf_ref[slot].astype(BF16)
    wub_ref[0] = wuf_ref[slot].astype(BF16)
    wdb_ref[...] = wdf_ref[slot].astype(BF16)
    _ffn_step(k, nk, x_ref, g_ref, wgb_ref, wub_ref, wdb_ref, o_ref, xn_ref)


def _ffn_head(x, g, wg, wu, wd):
    tm, tf = TILES["ffn_head"]
    assert tf == MXU_COLS
    d = x.shape[1]
    dff = wg.shape[1]
    assert dff // tf >= HEAD_SLOTS
    blocks = 2 * (tm * d * 4) * 2 + tm * d * 2 + 3 * d * tf * (HEAD_SLOTS * 4 + 2 * 2)
    temps = 3 * tm * tf * 4 + tm * d * 4
    row = lambda k: (k, 0)
    tile = lambda k: (k, 0, 0)
    return pl.pallas_call(
        _ffn_head_kernel,
        grid=(dff // tf,),
        in_specs=[
            pl.BlockSpec((tm, d), lambda k: (0, 0)),
            _const_spec((1, d)),
            pl.BlockSpec(memory_space=pl.ANY),
            pl.BlockSpec(memory_space=pl.ANY),
            pl.BlockSpec(memory_space=pl.ANY),
        ],
        out_specs=[
            pl.BlockSpec((tm, d), lambda k: (0, 0)),
            pl.BlockSpec((1, d, tf), tile),
            pl.BlockSpec((1, d, tf), tile),
            pl.BlockSpec((tf, d), row),
        ],
        out_shape=[
            jax.ShapeDtypeStruct((tm, d), F32),
            jax.ShapeDtypeStruct((dff // tf, d, tf), BF16),
            jax.ShapeDtypeStruct((dff // tf, d, tf), BF16),
            jax.ShapeDtypeStruct(wd.shape, BF16),
        ],
        scratch_shapes=[
            pltpu.VMEM((tm, d), BF16),
            pltpu.VMEM((HEAD_SLOTS, d, tf), F32),
            pltpu.VMEM((HEAD_SLOTS, d, tf), F32),
            pltpu.VMEM((HEAD_SLOTS, tf, d), F32),
            pltpu.SemaphoreType.DMA((3, HEAD_SLOTS)),
        ],
        compiler_params=_params(blocks, temps, 1),
        name="ffn_head",
    )(x, g, wg, wu, wd)


def _cast_spec(shape, ni, nk):
    r, c = shape
    row_align = 2 * SUBLANES
    if c % (ni * LANES) == 0 and r % (nk * row_align) == 0:
        return pl.BlockSpec((r // nk, c // ni), lambda i, k: (k, i))
    m = max(m for m in range(1, nk + 1) if c % (m * LANES) == 0)
    assert r % (ni * row_align) == 0, shape
    return pl.BlockSpec((r // ni, c // m), lambda i, k: (i, jnp.minimum(k, m - 1)))


def _ffn(x, g, wg, wu, wd, gf, *, final_norm, side=(), side_tiled=(), head=None):
    tm, tf = TILES["ffn"]
    n, d = x.shape
    dff = wd.shape[0]
    ni, nk = n // tm, dff // tf
    pieces = tf // MXU_COLS
    adopt = head is not None
    if adopt:
        assert head.shape == (tm, d) and TILES["ffn_head"][0] == tm
        xi = lambda i: jnp.maximum(i, 1)
        wk = lambda i, k: jnp.where(i == 0, 0, k)
    else:
        xi = lambda i: i
        wk = lambda i, k: k
    side_specs = [_cast_spec(w.shape, ni, nk) for w in side]
    side_out_specs = list(side_specs)
    side_shapes = [jax.ShapeDtypeStruct(w.shape, BF16) for w in side]
    for w in side_tiled:
        r, c = w.shape
        assert c == dff and r % (ni * 2 * SUBLANES) == 0
        side_specs.append(pl.BlockSpec((r // ni, tf), lambda i, k: (i, k)))
        side_out_specs.append(pl.BlockSpec((pieces, r // ni, MXU_COLS), lambda i, k: (k, i, 0)))
        side_shapes.append(jax.ShapeDtypeStruct((c // MXU_COLS, r, MXU_COLS), BF16))
    n_side = len(side_specs)
    side_bytes = sum(2 * sp.block_shape[0] * sp.block_shape[1] * (4 + 2) for sp in side_specs)
    blocks = 2 * (tm * d * 4) * 2 + tm * d * 2 + 2 * 3 * d * tf * 2 + side_bytes
    temps = 3 * tm * tf * 4 + tm * d * 4
    outs = pl.pallas_call(
        functools.partial(_ffn_kernel, final_norm=final_norm, n_side=n_side, adopt=adopt),
        grid=(ni, nk),
        in_specs=[
            pl.BlockSpec((tm, d), lambda i, k: (xi(i), 0)),
            _const_spec((1, d)),
            pl.BlockSpec((pieces, d, MXU_COLS), lambda i, k: (wk(i, k), 0, 0)),
            pl.BlockSpec((pieces, d, MXU_COLS), lambda i, k: (wk(i, k), 0, 0)),
            pl.BlockSpec((tf, d), lambda i, k: (wk(i, k), 0)),
            _const_spec((1, d)),
        ] + ([pl.BlockSpec(memory_space=pl.ANY)] if adopt else []) + side_specs,
        out_specs=[pl.BlockSpec((tm, d), lambda i, k: (i, 0))] + side_out_specs,
        out_shape=[jax.ShapeDtypeStruct((n, d), F32)] + side_shapes,
        scratch_shapes=[pltpu.VMEM((tm, d), BF16)] + ([pltpu.SemaphoreType.DMA(())] if adopt else []),
        compiler_params=_params(blocks, temps, 2),
        name="ffn",
    )(x, g, wg, wu, wd, gf, *([head] if adopt else []), *side, *side_tiled)
    return outs[0], list(outs[1:])


def _s5_prep_kernel(lr_ref, li_ref, ldt_ref, br_ref, bi_ref, cr_ref, ci_ref,
                    ar_ref, ai_ref, rb_ref, rc_ref, bbr_ref, bbi_ref, *, h):
    lr = lr_ref[...]
    li = li_ref[...]
    g, p = lr.shape
    dt = jnp.exp(ldt_ref[...])
    mag = jnp.exp(lr * dt)
    abar_re = mag * jnp.cos(li * dt)
    abar_im = mag * jnp.sin(li * dt)
    nr, ni = abar_re - 1.0, abar_im
    den = lr * lr + li * li
    fr = (nr * lr + ni * li) / den
    fi = (ni * lr - nr * li) / den
    ar_ref[...] = abar_re
    ai_ref[...] = abar_im
    for gg in range(g):
        rows = slice(gg * h, (gg + 1) * h)
        frg, fig = fr[gg:gg + 1, :], fi[gg:gg + 1, :]
        br, bi = br_ref[rows, :], bi_ref[rows, :]
        bbr_ref[rows, :] = frg * br - fig * bi
        bbi_ref[rows, :] = frg * bi + fig * br

    gl = LANES // h
    half = gl * p
    iota = lax.broadcasted_iota
    spread = (iota(jnp.int32, (p, half), 1) % p == iota(jnp.int32, (p, half), 0)).astype(BF16)
    spread_t = (iota(jnp.int32, (half, p), 0) % p == iota(jnp.int32, (half, p), 1)).astype(BF16)
    diag_b = iota(jnp.int32, (LANES, half), 0) // h == iota(jnp.int32, (LANES, half), 1) // p
    diag_c = iota(jnp.int32, (half, LANES), 0) // p == iota(jnp.int32, (half, LANES), 1) // h
    for j in range(g // gl):
        jj, j2 = divmod(j, 2)
        rows = slice(j * LANES, (j + 1) * LANES)
        for ri, (b_ref, c_ref, sign) in enumerate(((bbr_ref, cr_ref, 1.0), (bbi_ref, ci_ref, -1.0))):
            blk = jnp.dot(b_ref[rows, :].astype(BF16), spread, preferred_element_type=F32)
            rb_ref[jj, j2 * LANES:(j2 + 1) * LANES, ri * half:(ri + 1) * half] = (
                jnp.where(diag_b, blk, 0.0).astype(BF16))
            blk = lax.dot_general(spread_t, c_ref[rows, :].astype(BF16), (((1,), (1,)), ((), ())),
                                  preferred_element_type=F32)
            rc_ref[jj, ri * half:(ri + 1) * half, j2 * LANES:(j2 + 1) * LANES] = (
                jnp.where(diag_c, sign * blk, 0.0).astype(BF16))


def _s5_prep(a_re, a_im, log_dt, b_re, b_im, c_re, c_im):
    g, p, h = b_re.shape
    gl = LANES // h
    npair = g // (2 * gl)
    rows_gh = lambda a: a.reshape(g * h, p)
    return pl.pallas_call(
        functools.partial(_s5_prep_kernel, h=h),
        out_shape=(
            jax.ShapeDtypeStruct((g, p), F32),
            jax.ShapeDtypeStruct((g, p), F32),
            jax.ShapeDtypeStruct((npair, 2 * LANES, 2 * gl * p), BF16),
            jax.ShapeDtypeStruct((npair, 2 * gl * p, 2 * LANES), BF16),
        ),
        scratch_shapes=[pltpu.VMEM((g * h, p), F32), pltpu.VMEM((g * h, p), F32)],
        name="s5_prep",
    )(a_re, a_im, log_dt.reshape(g, 1),
      rows_gh(jnp.swapaxes(b_re, 1, 2)), rows_gh(jnp.swapaxes(b_im, 1, 2)), rows_gh(c_re), rows_gh(c_im))


def _mixer_kernel(h0_ref, hn_ref, gmix_ref, win_ref, rb_ref, rc_ref, at_ref, d_ref, wglu_ref, bglu_ref,
                  gs_ref, wp_ref, sc_ref, gp_ref, o_ref,
                  u_ref, ul_ref, s_ref, y_ref, yn_ref, st_ref, ext_ref, du_ref, xn_ref, *,
                  tc, nb, npair):
    c = pl.program_id(0)
    d = hn_ref.shape[-1]
    w = d // 2
    half = s_ref.shape[-1] // 2
    rows_per_t = 2 * nb
    nbuf = s_ref.shape[0]

    def in_proj(h_ref):
        h = h_ref[...].reshape(nb * tc, d)
        return jnp.dot(_rms(h, gmix_ref[...]).astype(BF16), win_ref[...], preferred_element_type=F32)

    @pl.when(c == 0)
    def _():
        ul_ref[...] = jnp.zeros(ul_ref.shape, F32)
        st_ref[...] = jnp.zeros(st_ref.shape, F32)
        ext_ref[:, :POOL_HALO, :] = jnp.zeros((nb, POOL_HALO, w), F32)
        u_ref[...] = in_proj(h0_ref)

    for jj in range(npair):
        for j2 in range(2):
            j = 2 * jj + j2
            for b in range(nb):
                ul_ref[jj, j2, pl.ds(b * 2 + j2, tc, stride=rows_per_t), :] = (
                    u_ref[b * tc:(b + 1) * tc, j * LANES:(j + 1) * LANES])
    du_ref[...] = d_ref[...] * u_ref[:, :w]
    for b in range(nb):
        ext_ref[b, POOL_HALO:, :] = u_ref[b * tc:(b + 1) * tc, w:]

    xn_ref[...] = _rms(hn_ref[...].reshape(nb * tc, d), gmix_ref[...]).astype(BF16)
    pw = d // (2 * npair)
    gw = w // len(POOL_WINDOWS)
    t_idx = c * tc + lax.broadcasted_iota(jnp.int32, (tc, 1), 0)

    def next_u(p):
        u_ref[:, p * pw:(p + 1) * pw] = jnp.dot(
            xn_ref[...], win_ref[:, p * pw:(p + 1) * pw], preferred_element_type=F32)

    def b_proj(jj):
        lhs = jnp.concatenate([ul_ref[jj, 0], ul_ref[jj, 1]], axis=-1).astype(BF16)
        s_ref[jj % nbuf] = jnp.dot(lhs, rb_ref[jj], preferred_element_type=F32)

    def scan(jj):
        sb = jj % nbuf
        a_r = at_ref[jj, :, :half]
        a_i = at_ref[jj, :, half:]
        s_r = st_ref[jj, :, :half]
        s_i = st_ref[jj, :, half:]
        for t in range(tc):
            r0 = t * rows_per_t
            n_r = a_r * s_r - a_i * s_i + s_ref[sb, r0:r0 + rows_per_t, :half]
            n_i = a_r * s_i + a_i * s_r + s_ref[sb, r0:r0 + rows_per_t, half:]
            s_ref[sb, r0:r0 + rows_per_t, :half] = n_r
            s_ref[sb, r0:r0 + rows_per_t, half:] = n_i
            s_r, s_i = n_r, n_i
        st_ref[jj, :, :half] = s_r
        st_ref[jj, :, half:] = s_i

    def c_proj(jj):
        sb = jj % nbuf
        yy = jnp.dot(s_ref[sb].astype(BF16), rc_ref[jj], preferred_element_type=F32)
        y_ref[sb, 0] = yy[:, :LANES]
        y_ref[sb, 1] = yy[:, LANES:]
        for j2 in range(2):
            j = 2 * jj + j2
            for b in range(nb):
                yn_ref[b * tc:(b + 1) * tc, j * LANES:(j + 1) * LANES] = (
                    y_ref[sb, j2, pl.ds(b * 2 + j2, tc, stride=rows_per_t), :])

    def s5_out():
        y = yn_ref[...] + du_ref[...]
        y = jax.nn.gelu(y)
        z = jnp.dot(y.astype(BF16), wglu_ref[...], preferred_element_type=F32) + bglu_ref[...]
        y = y * jax.nn.sigmoid(z)
        o_ref[:, :, :w] = _rms(y, gs_ref[...]).astype(BF16).reshape(nb, tc, w)

    def pool(b):
        zs = []
        ssq = jnp.zeros((tc, 1), F32)
        for gi, win in enumerate(POOL_WINDOWS):
            e = ext_ref[b, :, gi * gw:(gi + 1) * gw]
            s = e
            shift = 1
            while shift < win:
                s = s + pltpu.roll(s, shift, axis=0)
                shift *= 2
            cnt = jnp.minimum(t_idx + 1, win).astype(F32)
            pooled = s[POOL_HALO:] / cnt - e[POOL_HALO:]
            z = jnp.dot(pooled.astype(BF16), wp_ref[gi], preferred_element_type=F32)
            z = z * sc_ref[:, gi * gw:(gi + 1) * gw]
            ssq = ssq + jnp.sum(z * z, axis=-1, keepdims=True)
            zs.append(z)
        inv = lax.rsqrt(ssq / w + EPS)
        for gi in range(len(POOL_WINDOWS)):
            o_ref[b, :, w + gi * gw:w + (gi + 1) * gw] = (
                zs[gi] * inv * gp_ref[:, gi * gw:(gi + 1) * gw]).astype(BF16)
        ext_ref[b, :POOL_HALO, :] = ext_ref[b, tc:, :]

    for jj in range(min(nbuf, npair)):
        b_proj(jj)
    for jj in range(npair):
        next_u(2 * jj)
        scan(jj)
        c_proj(jj)
        if jj < npair - 1:
            next_u(2 * jj + 1)
        for b in range(jj * nb // npair, (jj + 1) * nb // npair):
            pool(b)
        if jj + nbuf < npair:
            b_proj(jj + nbuf)
    next_u(2 * npair - 1)
    s5_out()


def _mixer(h3, gmix, win, rb, rc, at, dskip, wglu, bglu, gs, wp, sc, gp):
    tc = TILES["mixer"]
    nb, L, d = h3.shape
    w = d // 2
    npair = rb.shape[0]
    nstate = rb.shape[2]
    rows = tc * 2 * nb
    nbuf = 2
    scratch = [
        pltpu.VMEM((nb * tc, d), F32),
        pltpu.VMEM((npair, 2, rows, LANES), F32),
        pltpu.VMEM((nbuf, rows, nstate), F32),
        pltpu.VMEM((nbuf, 2, rows, LANES), F32),
        pltpu.VMEM((nb * tc, w), F32),
        pltpu.VMEM((npair, 2 * nb, nstate), F32),
        pltpu.VMEM((nb, tc + POOL_HALO, w), F32),
        pltpu.VMEM((nb * tc, w), F32),
        pltpu.VMEM((nb * tc, d), BF16),
    ]
    blocks = (3 * nb * tc * d * 4 + 3 * nb * tc * d * 2 + d * d * 2 + rb.size * 2 + rc.size * 2
              + at.size * 4 + w * w * 2 + wp.size * 2
              + (nb * tc * d + npair * 2 * rows * LANES + nbuf * rows * nstate + nbuf * 2 * rows * LANES
                 + 2 * nb * tc * w + npair * 2 * nb * nstate + nb * (tc + POOL_HALO) * w) * 4)
    temps = nb * tc * d * 6 + rows * nstate * 6 + 4 * nb * tc * w * 4
    nchunks = L // tc
    return pl.pallas_call(
        functools.partial(_mixer_kernel, tc=tc, nb=nb, npair=npair),
        grid=(nchunks,),
        in_specs=[
            pl.BlockSpec((nb, tc, d), lambda c: (0, 0, 0), pipeline_mode=pl.Buffered(1)),
            pl.BlockSpec((nb, tc, d), lambda c: (0, jnp.minimum(c + 1, nchunks - 1), 0)),
            _const_spec((1, d)),
            _const_spec((d, d)),
            _const_spec(rb.shape),
            _const_spec(rc.shape),
            _const_spec(at.shape),
            _const_spec((1, w)),
            _const_spec((w, w)),
            _const_spec((1, w)),
            _const_spec((1, w)),
            _const_spec(wp.shape),
            _const_spec((1, w)),
            _const_spec((1, w)),
        ],
        out_specs=pl.BlockSpec((nb, tc, d), lambda c: (0, c, 0)),
        out_shape=jax.ShapeDtypeStruct((nb, L, d), BF16),
        scratch_shapes=scratch,
        compiler_params=_params(blocks, temps, 1),
        name="mixer",
    )(h3, h3, gmix, win, rb, rc, at, dskip, wglu, bglu, gs, wp, sc, gp)


def _kv_kernel(m_ref, g_ref, wk_ref, wv_ref, k_ref, v_ref, mn_ref):
    @pl.when(pl.program_id(0) == 0)
    def _():
        mn_ref[...] = _rms(m_ref[...], g_ref[...]).astype(BF16)

    mn = mn_ref[...]
    k_ref[...] = jnp.dot(mn, wk_ref[...], preferred_element_type=F32).astype(BF16)
    v_ref[...] = jnp.dot(mn, wv_ref[...], preferred_element_type=F32).astype(BF16)


def _kv(mem2, g, wk, wv):
    tn = TILES["kv"]
    n, d = mem2.shape
    blocks = n * d * 4 + n * d * 2 + 4 * d * tn * 2 + 4 * n * tn * 2
    temps = 2 * n * tn * 4 + n * d * 4
    return pl.pallas_call(
        _kv_kernel,
        grid=(d // tn,),
        in_specs=[
            _const_spec((n, d)),
            _const_spec((1, d)),
            pl.BlockSpec((d, tn), lambda j: (0, j)),
            pl.BlockSpec((d, tn), lambda j: (0, j)),
        ],
        out_specs=(pl.BlockSpec((n, tn), lambda j: (0, j)), pl.BlockSpec((n, tn), lambda j: (0, j))),
        out_shape=(jax.ShapeDtypeStruct((n, d), BF16), jax.ShapeDtypeStruct((n, d), BF16)),
        scratch_shapes=[pltpu.VMEM((n, d), BF16)],
        compiler_params=_params(blocks, temps, 1),
        name="mem_kv",
    )(mem2, g, wk, wv)


def _xattn_kernel(h_ref, m_ref, wout_ref, g_ref, wq_ref, k_ref, v_ref, wo_ref, o_ref):
    o_ref[...] = h_ref[...] + jnp.dot(m_ref[...], wout_ref[...], preferred_element_type=F32)
    h = o_ref[...]
    d = h.shape[-1]
    hd = d // MEM_HEADS
    q = jnp.dot(_rms(h, g_ref[...]).astype(BF16), wq_ref[...], preferred_element_type=F32)
    outs = []
    for hh in range(MEM_HEADS):
        qh = q[:, hh * hd:(hh + 1) * hd].astype(BF16)
        kh = k_ref[:, hh * hd:(hh + 1) * hd]
        s = lax.dot_general(qh, kh, (((1,), (1,)), ((), ())), preferred_element_type=F32)
        s = s * (hd ** -0.5)
        e = jnp.exp(s - jnp.max(s, axis=-1, keepdims=True))
        p = e / jnp.sum(e, axis=-1, keepdims=True)
        outs.append(jnp.dot(p.astype(BF16), v_ref[:, hh * hd:(hh + 1) * hd],
                            preferred_element_type=F32).astype(BF16))
    o = jnp.concatenate(outs, axis=-1)
    o_ref[...] = h + jnp.dot(o, wo_ref[...], preferred_element_type=F32)


def _xattn(h, m, wout, g, wq, k, v, wo, *, nb):
    tm = TILES["xattn"]
    n, d = h.shape
    nm = k.shape[0] // nb
    tiles = n // nb // tm
    rows = lambda b, i: (b * tiles + i, 0)
    blocks = 4 * tm * d * 4 + 2 * tm * d * 2 + 3 * d * d * 2 + 4 * nm * d * 2
    temps = 3 * tm * d * 4
    return pl.pallas_call(
        _xattn_kernel,
        grid=(nb, tiles),
        in_specs=[
            pl.BlockSpec((tm, d), rows),
            pl.BlockSpec((tm, d), rows),
            _const_spec((d, d)),
            _const_spec((1, d)),
            _const_spec((d, d)),
            pl.BlockSpec((nm, d), lambda b, i: (b, 0)),
            pl.BlockSpec((nm, d), lambda b, i: (b, 0)),
            _const_spec((d, d)),
        ],
        out_specs=pl.BlockSpec((tm, d), rows),
        out_shape=jax.ShapeDtypeStruct((n, d), F32),
        compiler_params=_params(blocks, temps, 2),
        name="xattn",
    )(h, m, wout, g, wq, k, v, wo)


def _s5_pole_table(abar_re, abar_im, npair, nb):
    g, p = abar_re.shape
    gl = g // (2 * npair)
    ab = jnp.stack([abar_re, abar_im]).reshape(2, npair, 2, gl, p)
    at = jnp.transpose(ab, (1, 2, 0, 3, 4)).reshape(npair, 1, 2, 2 * gl * p)
    return jnp.broadcast_to(at, (npair, nb, 2, 2 * gl * p)).reshape(npair, 2 * nb, 2 * gl * p)


def kernel(x, mem, g_ffn1, w1_gate, w1_up, w1_down, g_mix, w_in, ssm_a_re, ssm_a_im, ssm_log_dt,
           ssm_b_re, ssm_b_im, ssm_c_re, ssm_c_im, ssm_d, w_glu, b_glu, w_pool, pool_scale,
           g_out_ssm, g_out_pool, w_out, g_xattn, g_mem, w_q, w_k, w_v, w_o,
           g_ffn2, w2_gate, w2_up, w2_down, g_final):
    nb, L, d = x.shape
    n = nb * L
    depth = g_ffn1.shape[0]
    bf = lambda a: a.astype(BF16)
    row = lambda a: a.reshape(1, -1)

    h = x.reshape(n, d)
    for l in range(depth):
        last = l == depth - 1
        pool_shape = w_pool[l].shape
        later = [w2_down[l], w_in[l], w_out[l], w_q[l], w_k[l], w_v[l], w_o[l],
                 w_glu[l], w_pool[l].reshape(-1, pool_shape[-1])]
        head, w1g, w1u, w1d = _ffn_head(h, row(g_ffn1[l]), w1_gate[l], w1_up[l], w1_down[l])
        h, later = _ffn(h, row(g_ffn1[l]), w1g, w1u, w1d, row(g_final), final_norm=False,
                        side=later, side_tiled=[w2_gate[l], w2_up[l]], head=head)
        w2d, win, wout, wq, wk, wv, wo, wglu, wpool, w2g, w2u = later

        abar_re, abar_im, rb, rc = _s5_prep(ssm_a_re[l], ssm_a_im[l], ssm_log_dt[l],
                                            ssm_b_re[l], ssm_b_im[l], ssm_c_re[l], ssm_c_im[l])
        at = _s5_pole_table(abar_re, abar_im, rb.shape[0], nb)
        merged = _mixer(h.reshape(nb, L, d), row(g_mix[l]), win, rb, rc, at, row(ssm_d[l]),
                        wglu, row(b_glu[l]), row(g_out_ssm[l]),
                        wpool.reshape(pool_shape), row(pool_scale[l]), row(g_out_pool[l]))
        k, v = _kv(mem.reshape(nb * mem.shape[1], d), row(g_mem[l]), wk, wv)
        h = _xattn(h, merged.reshape(n, d), wout, row(g_xattn[l]), wq, k, v, wo, nb=nb)

        h, _ = _ffn(h, row(g_ffn2[l]), w2g, w2u, w2d, row(g_final), final_norm=last)
    return h.reshape(nb, L, d)
```
